```python
import jax
import jax.numpy as jnp
from jax import lax
import numpy as np

D_MODEL = 2048
BATCH = 4
SEQ = 2048
DEPTH = 4
DEC_BATCH = 128
DEC_SEQ = 4
PAST_LEN = 8192
PAGE_SIZE = 128

N_MIXERS = 3
N_A = (DEPTH + 2) // N_MIXERS
N_B = (DEPTH + 1) // N_MIXERS
N_C = DEPTH // N_MIXERS
Q_BLOCK = 128
NORM_EPS = 1e-6

MLA_NOPE = 128
MLA_ROPE = 64
MLA_V = 128
MLA_HEADS = D_MODEL // MLA_V
MLA_Q_LORA = D_MODEL // 4
MLA_KV_LORA = D_MODEL // 4
MLA_SCALE = (MLA_NOPE + MLA_ROPE) ** -0.5
ROPE_THETA = 10000.0
MLA_IN = MLA_Q_LORA + MLA_KV_LORA + MLA_ROPE + D_MODEL

FOX_HEAD_DIM = 128
FOX_HEADS = D_MODEL // FOX_HEAD_DIM
FOX_KV_HEADS = 4
FOX_GROUP = FOX_HEADS // FOX_KV_HEADS
FOX_SCALE = FOX_HEAD_DIM ** -0.5
FOX_FORGET_BIAS = 4.0
FOX_IN = (FOX_HEADS + 2 * FOX_KV_HEADS) * FOX_HEAD_DIM + FOX_HEADS + D_MODEL

RWKV_HEAD = 64
RWKV_HEADS = D_MODEL // RWKV_HEAD
RWKV_DECAY_LORA = 96
RWKV_A_LORA = 96
RWKV_LNX_EPS = 64e-5
RWKV_N_MIX = 6

kernel_name = 'hybrid_mla_fox_rwkv7_step'


def rms_norm(x, g):
    xf = x.astype(jnp.float32)
    y = xf * lax.rsqrt(jnp.mean(xf * xf, axis=-1, keepdims=True) + NORM_EPS)
    return (y * g.astype(jnp.float32)).astype(x.dtype)


def rope(x, pos):
    half = x.shape[-1] // 2
    inv = ROPE_THETA ** (-jnp.arange(half, dtype=jnp.float32) / half)
    ang = pos[:, None] * inv[None, :]
    ang = ang.reshape((1, ang.shape[0]) + (1,) * (x.ndim - 3) + (half,))
    cos, sin = jnp.cos(ang), jnp.sin(ang)
    xf = x.astype(jnp.float32)
    x1, x2 = xf[..., :half], xf[..., half:]
    return jnp.concatenate([x1 * cos - x2 * sin, x1 * sin + x2 * cos], axis=-1).astype(x.dtype)


def gather_pages(pool, layer, page_table):
    g = pool[layer, page_table]
    return g.reshape((page_table.shape[0], page_table.shape[1] * pool.shape[2]) + pool.shape[3:])


def causal_block_sweep(core, q_args, k_args, seq_len):
    outs = []
    for blk in range(seq_len // Q_BLOCK):
        q0 = blk * Q_BLOCK
        k_end = q0 + Q_BLOCK
        mask = (q0 + jnp.arange(Q_BLOCK))[:, None] >= jnp.arange(k_end)[None, :]
        outs.append(core(*[a[:, q0:k_end] for a in q_args], *[a[:, :k_end] for a in k_args], mask))
    return jnp.concatenate(outs, axis=1)


def sample_mask(n_new, n_past):
    return jnp.arange(n_past + n_new)[None, :] <= (n_past + jnp.arange(n_new))[:, None]


def mla_core(q_lat, q_pe, c_kv, k_pe, mask):
    s = jnp.einsum('bqhl,bkl->bhqk', q_lat, c_kv) + jnp.einsum('bqhr,bkr->bhqk', q_pe, k_pe)
    s = jnp.where(mask, s.astype(jnp.float32) * MLA_SCALE, -jnp.inf)
    p = jax.nn.softmax(s, axis=-1).astype(c_kv.dtype)
    return jnp.einsum('bhqk,bkl->bqhl', p, c_kv)


def mla_mixer(h, pos, past_ckv, past_kpe, w_in, q_norm, kv_norm, w_uq, w_ukv):
    bsz, t, _ = h.shape
    z = h @ w_in
    cq, ckv, kpe, gate = jnp.split(z, [MLA_Q_LORA, MLA_Q_LORA + MLA_KV_LORA, MLA_Q_LORA + MLA_KV_LORA + MLA_ROPE], axis=-1)
    cq = rms_norm(cq, q_norm)
    ckv = rms_norm(ckv, kv_norm)
    kpe = rope(kpe, pos)
    q = (cq @ w_uq).reshape(bsz, t, MLA_HEADS, MLA_NOPE + MLA_ROPE)
    q_pe = rope(q[..., MLA_NOPE:], pos)
    q_lat = jnp.einsum('bthn,lhn->bthl', q[..., :MLA_NOPE], w_ukv[..., :MLA_NOPE])
    if past_ckv is None:
        o_lat = causal_block_sweep(mla_core, (q_lat, q_pe), (ckv, kpe), t)
    else:
        n_past = past_ckv.shape[1]
        keys_c = jnp.concatenate([past_ckv.astype(ckv.dtype), ckv], axis=1)
        keys_r = jnp.concatenate([past_kpe.astype(kpe.dtype), kpe], axis=1)
        o_lat = mla_core(q_lat, q_pe, keys_c, keys_r, sample_mask(t, n_past))
    u = jnp.einsum('bthl,lhv->bthv', o_lat, w_ukv[..., MLA_NOPE:]).reshape(bsz, t, MLA_HEADS * MLA_V)
    return u, gate, ckv, kpe


def fox_core(q, c_q, k, v, c_k, mask):
    bsz, nq = q.shape[:2]
    nk = k.shape[1]
    s = jnp.einsum('bqhgd,bshd->bhgqs', q, k).astype(jnp.float32) * FOX_SCALE
    cq = c_q.reshape(bsz, nq, FOX_KV_HEADS, FOX_GROUP).transpose(0, 2, 3, 1)[..., :, None]
    ck = c_k.reshape(bsz, nk, FOX_KV_HEADS, FOX_GROUP).transpose(0, 2, 3, 1)[..., None, :]
    s = jnp.where(mask, s + (cq - ck), -jnp.inf)
    p = jax.nn.softmax(s, axis=-1).astype(v.dtype)
    o = jnp.einsum('bhgqs,bshd->bqhgd', p, v)
    return o.reshape(bsz, nq, FOX_HEADS * FOX_HEAD_DIM)


def fox_mixer(h, past_k, past_v, past_logf, w_in, b_f, q_norm, k_norm):
    bsz, t, _ = h.shape
    nq = FOX_HEADS * FOX_HEAD_DIM
    nkv = FOX_KV_HEADS * FOX_HEAD_DIM
    z = h @ w_in
    q, k, v, fl, gate = jnp.split(z, [nq, nq + nkv, nq + 2 * nkv, nq + 2 * nkv + FOX_HEADS], axis=-1)
    q = rms_norm(q.reshape(bsz, t, FOX_KV_HEADS, FOX_GROUP, FOX_HEAD_DIM), q_norm)
    k = rms_norm(k.reshape(bsz, t, FOX_KV_HEADS, FOX_HEAD_DIM), k_norm)
    v = v.reshape(bsz, t, FOX_KV_HEADS, FOX_HEAD_DIM)
    logf = jax.nn.log_sigmoid(fl.astype(jnp.float32) + b_f.astype(jnp.float32))
    if past_k is None:
        c = jnp.cumsum(logf, axis=1)
        u = causal_block_sweep(fox_core, (q, c), (k, v, c), t)
    else:
        n_past = past_k.shape[1]
        c = jnp.cumsum(jnp.concatenate([past_logf.astype(jnp.float32), logf], axis=1), axis=1)
        k_all = jnp.concatenate([past_k.astype(k.dtype), k], axis=1)
        v_all = jnp.concatenate([past_v.astype(v.dtype), v], axis=1)
        u = fox_core(q, c[:, n_past:], k_all, v_all, c, sample_mask(t, n_past))
    return u, gate, k, v, logf


def _rwkv_step(state, inp):
    r, w, k, v, a, b = inp
    sa = jnp.einsum('bhvk,bhk->bhv', state, a)
    state = state * w[:, :, None, :] + sa[..., None] * b[:, :, None, :] + v[..., None] * k[:, :, None, :]
    return state, jnp.einsum('bhvk,bhk->bhv', state, r)


def rwkv_mixer(h, shift0, s0, mu, w_rkvg, w0, w1, w2, a0, a1, a2, k_k, k_a, r_k, lnx_w, lnx_b):
    bsz, t, _ = h.shape
    f32 = jnp.float32

    def heads(z):
        return z.reshape(bsz, t, RWKV_HEADS, RWKV_HEAD)

    h_prev = jnp.concatenate([shift0[:, None, :].astype(h.dtype), h[:, :-1]], axis=1)
    xm = h[None] + (h_prev - h)[None] * mu[:, None, None, :]
    r, k, v, g = jnp.einsum('pbtd,pde->pbte', xm[:4], w_rkvg)
    w_log = -jax.nn.softplus(-(w0 + jnp.tanh(xm[4] @ w1) @ w2).astype(f32)) - 0.5
    decay = jnp.exp(-jnp.exp(w_log))
    a = jax.nn.sigmoid((a0 + (xm[5] @ a1) @ a2).astype(f32))
    kk = heads((k * k_k).astype(f32))
    kk = kk * lax.rsqrt(jnp.maximum(jnp.sum(kk * kk, axis=-1, keepdims=True), 1e-24))
    a4 = heads(a)
    k4 = heads(k.astype(f32) * (1.0 + (a - 1.0) * k_a.astype(f32)))
    r4 = heads(r.astype(f32))
    v4 = heads(v.astype(f32))
    xs = tuple(jnp.moveaxis(z, 1, 0) for z in (r4, heads(decay), k4, v4, -kk, kk * a4))
    s_last, y = lax.scan(_rwkv_step, s0.astype(f32), xs)
    y = jnp.moveaxis(y, 0, 1)
    mean = jnp.mean(y, axis=-1, keepdims=True)
    var = jnp.mean(jnp.square(y - mean), axis=-1, keepdims=True)
    y = ((y - mean) * lax.rsqrt(var + RWKV_LNX_EPS)).reshape(bsz, t, D_MODEL) * lnx_w + lnx_b
    bonus = jnp.sum(r4 * k4 * r_k, axis=-1, keepdims=True) * v4
    u = (y + bonus.reshape(bsz, t, D_MODEL)).astype(h.dtype)
    return u, g, s_last, h[:, -1]


def trunk(x, pos, past, page_table, p):
    bsz = x.shape[0]
    new_ckv, new_kpe, new_k, new_v, new_logf, new_wkv, new_shift = [], [], [], [], [], [], []
    for i in range(DEPTH):
        j = i // N_MIXERS
        kind = i % N_MIXERS
        h = rms_norm(x, p['norm_pre'][i])
        if kind == 0:
            pc = None if past is None else gather_pages(past['ckv'], j, page_table)
            pr = None if past is None else gather_pages(past['kpe'], j, page_table)
            u, gate, ckv, kpe = mla_mixer(h, pos, pc, pr, p['mla_w_in'][j], p['mla_q_norm'][j],
                                          p['mla_kv_norm'][j], p['mla_w_uq'][j], p['mla_w_ukv'][j])
            w_o = p['mla_w_o'][j]
            new_ckv.append(ckv)
            new_kpe.append(kpe)
        elif kind == 1:
            pk = None if past is None else gather_pages(past['k'], j, page_table)
            pv = None if past is None else gather_pages(past['v'], j, page_table)
            pl = None if past is None else gather_pages(past['logf'], j, page_table)
            u, gate, k, v, logf = fox_mixer(h, pk, pv, pl, p['fox_w_in'][j], p['fox_b_f'][j],
                                            p['fox_q_norm'][j], p['fox_k_norm'][j])
            w_o = p['fox_w_o'][j]
            new_k.append(k)
            new_v.append(v)
            new_logf.append(logf)
        else:
            if past is None:
                shift0 = jnp.zeros((bsz, D_MODEL), x.dtype)
                s0 = jnp.zeros((bsz, RWKV_HEADS, RWKV_HEAD, RWKV_HEAD), jnp.float32)
            else:
                shift0 = past['shift'][j]
                s0 = past['wkv'][j]
            u, gate, s_last, sh = rwkv_mixer(h, shift0, s0, p['rwkv_mu'][j], p['rwkv_w_rkvg'][j],
                                             p['rwkv_w0'][j], p['rwkv_w1'][j], p['rwkv_w2'][j],
                                             p['rwkv_a0'][j], p['rwkv_a1'][j], p['rwkv_a2'][j],
                                             p['rwkv_k_k'][j], p['rwkv_k_a'][j], p['rwkv_r_k'][j],
                                             p['rwkv_lnx_w'][j], p['rwkv_lnx_b'][j])
            w_o = p['rwkv_w_o'][j]
            new_wkv.append(s_last)
            new_shift.append(sh)
        y = (u * jax.nn.silu(gate)) @ w_o
        x = x + rms_norm(y, p['norm_post'][i])
    states = (jnp.stack(new_ckv), jnp.stack(new_kpe), jnp.stack(new_k), jnp.stack(new_v),
              jnp.stack(new_logf), jnp.stack(new_wkv), jnp.stack(new_shift))
    return x, states


def setup_inputs(seed: int = 0) -> dict:
    key = jax.random.key(seed)
    keys = iter(jax.random.split(key, 64))

    def normal(shape, scale=1.0):
        return scale * jax.random.normal(next(keys), shape, jnp.float32)

    def raw_normal(shape):
        return jax.random.normal(next(keys), shape, jnp.float32)

    def gain(shape):
        return 1.0 + 0.1 * jax.random.normal(next(keys), shape, jnp.float32)

    n_pages = PAST_LEN // PAGE_SIZE
    n_pool = (DEC_BATCH * n_pages * 5) // 4
    page_table = jax.random.permutation(next(keys), n_pool)[: DEC_BATCH * n_pages]
    page_table = page_table.reshape(DEC_BATCH, n_pages).astype(jnp.int32)
    d_inv = D_MODEL ** -0.5
    return {
        'x_prompt': raw_normal((BATCH, SEQ, D_MODEL)),
        'x_sample': raw_normal((DEC_BATCH, DEC_SEQ, D_MODEL)),
        'cache_mla_ckv': raw_normal((N_A, n_pool, PAGE_SIZE, MLA_KV_LORA)),
        'cache_mla_kpe': raw_normal((N_A, n_pool, PAGE_SIZE, MLA_ROPE)),
        'cache_fox_k': raw_normal((N_B, n_pool, PAGE_SIZE, FOX_KV_HEADS, FOX_HEAD_DIM)),
        'cache_fox_v': raw_normal((N_B, n_pool, PAGE_SIZE, FOX_KV_HEADS, FOX_HEAD_DIM)),
        'cache_fox_logf': jax.nn.log_sigmoid(FOX_FORGET_BIAS + raw_normal((N_B, n_pool, PAGE_SIZE, FOX_HEADS))),
        'state_rwkv_wkv': normal((N_C, DEC_BATCH, RWKV_HEADS, RWKV_HEAD, RWKV_HEAD), 0.3),
        'state_rwkv_shift': raw_normal((N_C, DEC_BATCH, D_MODEL)),
        'page_table': page_table,
        'norm_pre': gain((DEPTH, D_MODEL)),
        'norm_post': gain((DEPTH, D_MODEL)),
        'mla_w_in': normal((N_A, D_MODEL, MLA_IN), d_inv),
        'mla_q_norm': gain((N_A, MLA_Q_LORA)),
        'mla_kv_norm': gain((N_A, MLA_KV_LORA)),
        'mla_w_uq': normal((N_A, MLA_Q_LORA, MLA_HEADS * (MLA_NOPE + MLA_ROPE)), MLA_Q_LORA ** -0.5),
        'mla_w_ukv': normal((N_A, MLA_KV_LORA, MLA_HEADS, MLA_NOPE + MLA_V), MLA_KV_LORA ** -0.5),
        'mla_w_o': normal((N_A, D_MODEL, D_MODEL), d_inv),
        'fox_w_in': normal((N_B, D_MODEL, FOX_IN), d_inv),
        'fox_b_f': FOX_FORGET_BIAS + normal((N_B, FOX_HEADS), 0.1),
        'fox_q_norm': gain((N_B, FOX_HEAD_DIM)),
        'fox_k_norm': gain((N_B, FOX_HEAD_DIM)),
        'fox_w_o': normal((N_B, D_MODEL, D_MODEL), d_inv),
        'rwkv_mu': jax.random.uniform(next(keys), (N_C, RWKV_N_MIX, D_MODEL), jnp.float32),
        'rwkv_w_rkvg': normal((N_C, 4, D_MODEL, D_MODEL), d_inv),
        'rwkv_w0': -2.0 + normal((N_C, D_MODEL), 0.3),
        'rwkv_w1': normal((N_C, D_MODEL, RWKV_DECAY_LORA), d_inv),
        'rwkv_w2': normal((N_C, RWKV_DECAY_LORA, D_MODEL), 0.1 * RWKV_DECAY_LORA ** -0.5),
        'rwkv_a0': normal((N_C, D_MODEL), 0.1),
        'rwkv_a1': normal((N_C, D_MODEL, RWKV_A_LORA), d_inv),
        'rwkv_a2': normal((N_C, RWKV_A_LORA, D_MODEL), 0.1 * RWKV_A_LORA ** -0.5),
        'rwkv_k_k': gain((N_C, D_MODEL)),
        'rwkv_k_a': gain((N_C, D_MODEL)),
        'rwkv_r_k': normal((N_C, RWKV_HEADS, RWKV_HEAD), 0.1),
        'rwkv_lnx_w': gain((N_C, D_MODEL)),
        'rwkv_lnx_b': normal((N_C, D_MODEL), 0.02),
        'rwkv_w_o': normal((N_C, D_MODEL, D_MODEL), d_inv),
    }


def reference(x_prompt, x_sample, cache_mla_ckv, cache_mla_kpe, cache_fox_k, cache_fox_v, cache_fox_logf,
              state_rwkv_wkv, state_rwkv_shift, page_table, norm_pre, norm_post,
              mla_w_in, mla_q_norm, mla_kv_norm, mla_w_uq, mla_w_ukv, mla_w_o,
              fox_w_in, fox_b_f, fox_q_norm, fox_k_norm, fox_w_o,
              rwkv_mu, rwkv_w_rkvg, rwkv_w0, rwkv_w1, rwkv_w2, rwkv_a0, rwkv_a1, rwkv_a2,
              rwkv_k_k, rwkv_k_a, rwkv_r_k, rwkv_lnx_w, rwkv_lnx_b, rwkv_w_o):
    p = dict(norm_pre=norm_pre, norm_post=norm_post,
             mla_w_in=mla_w_in, mla_q_norm=mla_q_norm, mla_kv_norm=mla_kv_norm,
             mla_w_uq=mla_w_uq, mla_w_ukv=mla_w_ukv, mla_w_o=mla_w_o,
             fox_w_in=fox_w_in, fox_b_f=fox_b_f, fox_q_norm=fox_q_norm, fox_k_norm=fox_k_norm, fox_w_o=fox_w_o,
             rwkv_mu=rwkv_mu, rwkv_w_rkvg=rwkv_w_rkvg, rwkv_w0=rwkv_w0, rwkv_w1=rwkv_w1, rwkv_w2=rwkv_w2,
             rwkv_a0=rwkv_a0, rwkv_a1=rwkv_a1, rwkv_a2=rwkv_a2, rwkv_k_k=rwkv_k_k, rwkv_k_a=rwkv_k_a,
             rwkv_r_k=rwkv_r_k, rwkv_lnx_w=rwkv_lnx_w, rwkv_lnx_b=rwkv_lnx_b, rwkv_w_o=rwkv_w_o)
    pos_prompt = jnp.arange(x_prompt.shape[1], dtype=jnp.float32)
    y_prompt, (p_ckv, p_kpe, p_k, p_v, p_logf, p_wkv, p_shift) = trunk(x_prompt, pos_prompt, None, None, p)
    n_past = page_table.shape[1] * cache_mla_ckv.shape[2]
    pos_sample = n_past + jnp.arange(x_sample.shape[1], dtype=jnp.float32)
    past = dict(ckv=cache_mla_ckv, kpe=cache_mla_kpe, k=cache_fox_k, v=cache_fox_v, logf=cache_fox_logf,
                wkv=state_rwkv_wkv, shift=state_rwkv_shift)
    y_sample, (s_ckv, s_kpe, s_k, s_v, s_logf, s_wkv, s_shift) = trunk(x_sample, pos_sample, past, page_table, p)
    return (y_prompt, y_sample, p_ckv, p_kpe, p_k, p_v, p_logf, p_wkv, p_shift,
            s_ckv, s_kpe, s_k, s_v, s_logf, s_wkv, s_shift)
```

```python
import functools
import math

import jax
import jax.numpy as jnp
from jax import lax
from jax.experimental import pallas as pl
from jax.experimental.pallas import tpu as pltpu

F32 = jnp.float32
BF16 = jnp.bfloat16

D_MODEL = 2048
NORM_EPS = 1e-6
PAGE = 128

MLA_NOPE = 128
MLA_ROPE = 64
MLA_V = 128
MLA_HEADS = 16
MLA_HEAD_GROUP = 4
MLA_LORA = 512
MLA_SCALE = (MLA_NOPE + MLA_ROPE) ** -0.5
MLA_KW = MLA_LORA + 128
ROPE_THETA = 10000.0

FOX_HD = 128
FOX_HEADS = 16
FOX_KVH = 4
FOX_GROUP = 4
FOX_SCALE = FOX_HD ** -0.5

RWKV_HD = 64
RWKV_HEADS = 32
RWKV_LORA_PAD = 128
RWKV_LNX_EPS = 64e-5

TM = 256
TQ = 256
PAGES_PER_STEP = 8
NEG = -1e30
VMEM_LIMIT = 56 * 1024 * 1024
HI = lax.Precision.HIGHEST

NT = (((1,), (1,)), ((), ()))
TN = (((0,), (0,)), ((), ()))


def _params(sem):
    return pltpu.CompilerParams(dimension_semantics=sem, vmem_limit_bytes=VMEM_LIMIT)


def _resident(shape):
    nd = len(shape)
    return pl.BlockSpec(shape, lambda *_: (0,) * nd, pipeline_mode=pl.Buffered(1))


def _dot(a, b):
    return jnp.dot(a, b, preferred_element_type=F32)


def _dot_nt(a, b):
    return lax.dot_general(a, b, NT, preferred_element_type=F32)


def _rms(x, g):
    return x * lax.rsqrt(jnp.mean(x * x, axis=-1, keepdims=True) + NORM_EPS) * g


def _sigmoid(x):
    return 1.0 / (1.0 + jnp.exp(-x))


def _softplus(x):
    return jnp.maximum(x, 0.0) + jnp.log(1.0 + jnp.exp(-jnp.abs(x)))


def _mla_proj_kernel(x_ref, gpre_ref, w_ref, qn_ref, kvn_ref, wuqn_ref, wuqp_ref, wuqr_ref, wk_ref,
                     cos_ref, sin_ref, gate_ref, ckv_ref, kpe_ref, kcat_ref, qfull_ref):
    hb = _rms(x_ref[...], gpre_ref[...]).astype(BF16)
    cq = _rms(_dot(hb, w_ref[:, 0:512]), qn_ref[...]).astype(BF16)
    ckv = _rms(_dot(hb, w_ref[:, 512:1024]), kvn_ref[...])
    z = _dot(hb, w_ref[:, 1024:1152])
    cos_t = cos_ref[...]
    sin_t = sin_ref[...]
    lane = lax.broadcasted_iota(jnp.int32, z.shape, 1)
    kpe = jnp.where(lane < MLA_ROPE, z * cos_t + pltpu.roll(z, 64, 1) * sin_t, 0.0)
    gate_ref[...] = _dot(hb, w_ref[:, 1152:3200])
    ckv_ref[...] = ckv
    kpe_ref[...] = kpe[:, 0:MLA_ROPE]
    kcat_ref[:, 0:512] = ckv.astype(BF16)
    kcat_ref[:, 512:640] = kpe.astype(BF16)
    qn = _dot(cq, wuqn_ref[...]).astype(BF16)
    cos8 = jnp.concatenate([cos_t] * 8, axis=1)
    sin8 = jnp.concatenate([sin_t] * 8, axis=1)
    qpe = (_dot(cq, wuqp_ref[...]) * cos8 + _dot(cq, wuqr_ref[...]) * sin8) * MLA_SCALE
    for h in range(MLA_HEADS):
        ql = _dot(qn[:, h * 128:(h + 1) * 128], wk_ref[h]) * MLA_SCALE
        qfull_ref[h, :, 0:512] = ql.astype(BF16)
        blk = qpe[:, (h // 2) * 128:(h // 2 + 1) * 128]
        if h % 2:
            blk = pltpu.roll(blk, 64, 1)
        qfull_ref[h, :, 512:640] = jnp.where(lane < MLA_ROPE, blk, 0.0).astype(BF16)


def _mla_proj(x, gpre, w_main, qn, kvn, wuqn, wuqp, wuqr, wk, cos_t, sin_t):
    m = x.shape[0]
    row = lambda w: pl.BlockSpec((TM, w), lambda i: (i, 0))
    return pl.pallas_call(
        _mla_proj_kernel,
        grid=(m // TM,),
        in_specs=[row(D_MODEL), _resident((1, D_MODEL)), _resident(w_main.shape), _resident((1, 512)),
                  _resident((1, 512)), _resident(wuqn.shape), _resident(wuqp.shape), _resident(wuqr.shape),
                  _resident(wk.shape), row(128), row(128)],
        out_specs=[row(D_MODEL), row(512), row(MLA_ROPE), row(MLA_KW),
                   pl.BlockSpec((MLA_HEADS, TM, MLA_KW), lambda i: (0, i, 0))],
        out_shape=[jax.ShapeDtypeStruct((m, D_MODEL), F32), jax.ShapeDtypeStruct((m, 512), F32),
                   jax.ShapeDtypeStruct((m, MLA_ROPE), F32), jax.ShapeDtypeStruct((m, MLA_KW), BF16),
                   jax.ShapeDtypeStruct((MLA_HEADS, m, MLA_KW), BF16)],
        compiler_params=_params(("arbitrary",)),
        name="mla_proj",
    )(x, gpre, w_main, qn, kvn, wuqn, wuqp, wuqr, wk, cos_t, sin_t)


def _flash_update(s, v, m_ref, l_ref, acc_ref):
    m_prev = m_ref[...]
    m_new = jnp.maximum(m_prev, jnp.max(s, axis=-1, keepdims=True))
    alpha = jnp.exp(m_prev - m_new)
    p = jnp.exp(s - m_new)
    l_ref[...] = alpha * l_ref[...] + jnp.sum(p, axis=-1, keepdims=True)
    acc_ref[...] = alpha * acc_ref[...] + _dot(p.astype(BF16), v)
    m_ref[...] = m_new


def _flash_init(m_ref, l_ref, acc_ref):
    m_ref[...] = jnp.full(m_ref.shape, NEG, F32)
    l_ref[...] = jnp.zeros(l_ref.shape, F32)
    acc_ref[...] = jnp.zeros(acc_ref.shape, F32)


def _mla_attn_prompt_kernel(q_ref, k_ref, wv_ref, o_ref, acc_ref, m_ref, l_ref):
    qi = pl.program_id(1)
    rows = MLA_HEAD_GROUP * TQ
    q = q_ref[...].reshape(rows, MLA_KW)
    _flash_init(m_ref, l_ref, acc_ref)

    def block(kb, masked):
        k = k_ref[pl.ds(pl.multiple_of(kb * TQ, TQ), TQ), :]
        s = _dot_nt(q, k)
        if masked:
            tok = lax.broadcasted_iota(jnp.int32, s.shape, 0) & (TQ - 1)
            key = lax.broadcasted_iota(jnp.int32, s.shape, 1)
            s = jnp.where(key <= tok, s, NEG)
        _flash_update(s, k[:, 0:MLA_LORA], m_ref, l_ref, acc_ref)

    def body(kb, carry):
        block(kb, False)
        return carry

    lax.fori_loop(0, qi, body, 0)
    block(qi, True)
    o = (acc_ref[...] / l_ref[...]).astype(BF16)
    for h in range(MLA_HEAD_GROUP):
        o_ref[:, h * MLA_V:(h + 1) * MLA_V] = _dot(o[h * TQ:(h + 1) * TQ, :], wv_ref[h])


def _mla_attn_prompt(qfull, kcat, wv, n_batch, seq):
    nq = seq // TQ
    hg = MLA_HEAD_GROUP
    rows = hg * TQ
    return pl.pallas_call(
        _mla_attn_prompt_kernel,
        grid=(n_batch, nq, MLA_HEADS // hg),
        in_specs=[pl.BlockSpec((hg, TQ, MLA_KW), lambda b, q, g: (g, b * nq + q, 0)),
                  pl.BlockSpec((seq, MLA_KW), lambda b, q, g: (b, 0)),
                  pl.BlockSpec((hg, MLA_LORA, MLA_V), lambda b, q, g: (g, 0, 0))],
        out_specs=pl.BlockSpec((TQ, hg * MLA_V), lambda b, q, g: (b * nq + q, g)),
        out_shape=jax.ShapeDtypeStruct((n_batch * seq, D_MODEL), F32),
        scratch_shapes=[pltpu.VMEM((rows, MLA_LORA), F32), pltpu.VMEM((rows, 1), F32),
                        pltpu.VMEM((rows, 1), F32)],
        compiler_params=_params(("arbitrary", "arbitrary", "arbitrary")),
        name="mla_attn_prompt",
    )(qfull, kcat, wv)


def _mla_dec_kernel(pt_ref, q_ref, knew_ref, *rest, n_pages, n_new):
    g_pages = PAGES_PER_STEP
    ckv_refs = rest[:g_pages]
    kpe_refs = rest[g_pages:2 * g_pages]
    o_ref = rest[2 * g_pages]
    kbuf, m_ref, l_ref, acc_ref = rest[2 * g_pages + 1:]
    g = pl.program_id(1)

    @pl.when(g == 0)
    def _():
        _flash_init(m_ref, l_ref, acc_ref)
        kbuf[:, 512:640] = jnp.zeros((g_pages * PAGE, 128), BF16)

    for i in range(g_pages):
        kbuf[i * PAGE:(i + 1) * PAGE, 0:512] = ckv_refs[i][...].astype(BF16)
        kbuf[i * PAGE:(i + 1) * PAGE, 512:576] = kpe_refs[i][...].astype(BF16)
    q = q_ref[0]
    k = kbuf[...]
    _flash_update(_dot_nt(q, k), k[:, 0:MLA_LORA], m_ref, l_ref, acc_ref)

    @pl.when(g == n_pages // g_pages - 1)
    def _():
        kn = knew_ref[0]
        s = _dot_nt(q, kn)
        tok = lax.broadcasted_iota(jnp.int32, s.shape, 0) // MLA_HEADS
        key = lax.broadcasted_iota(jnp.int32, s.shape, 1)
        s = jnp.where((key <= tok) & (key < n_new), s, NEG)
        _flash_update(s, kn[:, 0:MLA_LORA], m_ref, l_ref, acc_ref)
        o = acc_ref[...] / l_ref[...]
        o_ref[0] = o[0:n_new * MLA_HEADS, :].reshape(n_new, MLA_HEADS, MLA_LORA)


def _mla_attn_sample(page_table, q_s, knew, cache_ckv, cache_kpe, layer, n_new):
    n_b, n_pages = page_table.shape
    g_pages = PAGES_PER_STEP
    pt = page_table.reshape(-1)

    def page_spec(width, i):
        return pl.BlockSpec((None, None, PAGE, width),
                            lambda b, g, pt_ref: (layer, pt_ref[b * n_pages + g * g_pages + i], 0, 0))

    grid_spec = pltpu.PrefetchScalarGridSpec(
        num_scalar_prefetch=1,
        grid=(n_b, n_pages // g_pages),
        in_specs=[pl.BlockSpec((1, 128, MLA_KW), lambda b, g, pt_ref: (b, 0, 0)),
                  pl.BlockSpec((1, 16, MLA_KW), lambda b, g, pt_ref: (b, 0, 0))]
                 + [page_spec(MLA_LORA, i) for i in range(g_pages)]
                 + [page_spec(MLA_ROPE, i) for i in range(g_pages)],
        out_specs=pl.BlockSpec((1, n_new, MLA_HEADS, MLA_LORA), lambda b, g, pt_ref: (b, 0, 0, 0)),
        scratch_shapes=[pltpu.VMEM((g_pages * PAGE, MLA_KW), BF16), pltpu.VMEM((128, 1), F32),
                        pltpu.VMEM((128, 1), F32), pltpu.VMEM((128, MLA_LORA), F32)],
    )
    return pl.pallas_call(
        functools.partial(_mla_dec_kernel, n_pages=n_pages, n_new=n_new),
        grid_spec=grid_spec,
        out_shape=jax.ShapeDtypeStruct((n_b, n_new, MLA_HEADS, MLA_LORA), F32),
        compiler_params=_params(("arbitrary", "arbitrary")),
        name="mla_attn_sample",
    )(pt, q_s, knew, *([cache_ckv] * g_pages), *([cache_kpe] * g_pages))


def _mla_up_kernel(o_ref, wv_ref, u_ref):
    for h in range(MLA_HEADS):
        oh = o_ref[:, h * MLA_LORA:(h + 1) * MLA_LORA].astype(BF16)
        u_ref[:, h * MLA_V:(h + 1) * MLA_V] = _dot(oh, wv_ref[h])


def _mla_up(o_lat, wv):
    m = o_lat.shape[0]
    return pl.pallas_call(
        _mla_up_kernel,
        grid=(m // TM,),
        in_specs=[pl.BlockSpec((TM, MLA_HEADS * MLA_LORA), lambda i: (i, 0)), _resident(wv.shape)],
        out_specs=pl.BlockSpec((TM, D_MODEL), lambda i: (i, 0)),
        out_shape=jax.ShapeDtypeStruct((m, D_MODEL), F32),
        compiler_params=_params(("arbitrary",)),
        name="mla_up",
    )(o_lat, wv)


def _head_rms(x, g, scale):
    return x * (lax.rsqrt(jnp.mean(x * x, axis=-1, keepdims=True) + NORM_EPS) * scale) * g


def _fox_proj_kernel(x_ref, gpre_ref, w_ref, bf_ref, qn_ref, kn_ref,
                     q_ref, k_ref, v_ref, kb_ref, vb_ref, lf_ref, c_ref, gate_ref, carry_ref, *, tiles_per_seq):
    i = pl.program_id(0)
    hb = _rms(x_ref[...], gpre_ref[...]).astype(BF16)
    q = _dot(hb, w_ref[:, 0:2048])
    for h in range(FOX_HEADS):
        sl = slice(h * FOX_HD, (h + 1) * FOX_HD)
        q_ref[:, sl] = _head_rms(q[:, sl], qn_ref[...], FOX_SCALE).astype(BF16)
    k = _dot(hb, w_ref[:, 2048:2560])
    for h in range(FOX_KVH):
        sl = slice(h * FOX_HD, (h + 1) * FOX_HD)
        kh = _head_rms(k[:, sl], kn_ref[...], 1.0)
        k_ref[:, sl] = kh
        kb_ref[:, sl] = kh.astype(BF16)
    v = _dot(hb, w_ref[:, 2560:3072])
    v_ref[...] = v
    vb_ref[...] = v.astype(BF16)
    fl = _dot(hb, w_ref[:, 3072:3200]) + bf_ref[...]
    lf = -_softplus(-fl)
    lf_ref[...] = lf[:, 0:FOX_HEADS]

    @pl.when(i % tiles_per_seq == 0)
    def _():
        carry_ref[...] = jnp.zeros(carry_ref.shape, F32)

    r = lax.broadcasted_iota(jnp.int32, (TM, TM), 0)
    c = lax.broadcasted_iota(jnp.int32, (TM, TM), 1)
    tri = (r >= c).astype(F32)
    cum = jnp.dot(tri, lf, preferred_element_type=F32, precision=HI) + carry_ref[...]
    c_ref[...] = cum[:, 0:FOX_HEADS]
    carry_ref[...] = cum[TM - 1:TM, :]
    gate_ref[...] = _dot(hb, w_ref[:, 3200:5248])


def _fox_proj(x, gpre, w_main, bf, qn, kn, tiles_per_seq):
    m = x.shape[0]
    row = lambda w: pl.BlockSpec((TM, w), lambda i: (i, 0))
    kvw = FOX_KVH * FOX_HD
    return pl.pallas_call(
        functools.partial(_fox_proj_kernel, tiles_per_seq=tiles_per_seq),
        grid=(m // TM,),
        in_specs=[row(D_MODEL), _resident((1, D_MODEL)), _resident(w_main.shape), _resident((1, 128)),
                  _resident((1, FOX_HD)), _resident((1, FOX_HD))],
        out_specs=[row(D_MODEL), row(kvw), row(kvw), row(kvw), row(kvw), row(FOX_HEADS), row(FOX_HEADS),
                   row(D_MODEL)],
        out_shape=[jax.ShapeDtypeStruct((m, D_MODEL), BF16), jax.ShapeDtypeStruct((m, kvw), F32),
                   jax.ShapeDtypeStruct((m, kvw), F32), jax.ShapeDtypeStruct((m, kvw), BF16),
                   jax.ShapeDtypeStruct((m, kvw), BF16), jax.ShapeDtypeStruct((m, FOX_HEADS), F32),
                   jax.ShapeDtypeStruct((m, FOX_HEADS), F32), jax.ShapeDtypeStruct((m, D_MODEL), F32)],
        scratch_shapes=[pltpu.VMEM((1, 128), F32)],
        compiler_params=_params(("arbitrary",)),
        name="fox_proj",
    )(x, gpre, w_main, bf, qn, kn)


def _fox_attn_prompt_kernel(q_ref, k_ref, v_ref, c_ref, o_ref, acc_ref, m_ref, l_ref):
    qi = pl.program_id(2)
    rows = FOX_GROUP * TQ
    q = jnp.concatenate([q_ref[:, g * FOX_HD:(g + 1) * FOX_HD] for g in range(FOX_GROUP)], axis=0)
    _flash_init(m_ref, l_ref, acc_ref)

    def block(kb, masked):
        start = pl.multiple_of(kb * TQ, TQ)
        k = k_ref[pl.ds(start, TQ), :]
        v = v_ref[pl.ds(start, TQ), :]
        cb = c_ref[0, 0, :, pl.ds(start, TQ)]
        s = (_dot_nt(q, k).reshape(FOX_GROUP, TQ, TQ) - cb[:, None, :]).reshape(rows, TQ)
        if masked:
            tok = lax.broadcasted_iota(jnp.int32, s.shape, 0) & (TQ - 1)
            key = lax.broadcasted_iota(jnp.int32, s.shape, 1)
            s = jnp.where(key <= tok, s, NEG)
        _flash_update(s, v, m_ref, l_ref, acc_ref)

    def body(kb, carry):
        block(kb, False)
        return carry

    lax.fori_loop(0, qi, body, 0)
    block(qi, True)
    o = acc_ref[...] / l_ref[...]
    for g in range(FOX_GROUP):
        o_ref[:, g * FOX_HD:(g + 1) * FOX_HD] = o[g * TQ:(g + 1) * TQ, :]


def _fox_attn_prompt(q, kb, vb, c_t, n_batch, seq):
    nq = seq // TQ
    rows = FOX_GROUP * TQ
    gw = FOX_GROUP * FOX_HD
    return pl.pallas_call(
        _fox_attn_prompt_kernel,
        grid=(n_batch, FOX_KVH, nq),
        in_specs=[pl.BlockSpec((TQ, gw), lambda b, h, qi: (b * nq + qi, h)),
                  pl.BlockSpec((seq, FOX_HD), lambda b, h, qi: (b, h)),
                  pl.BlockSpec((seq, FOX_HD), lambda b, h, qi: (b, h)),
                  pl.BlockSpec((1, 1, FOX_GROUP, seq), lambda b, h, qi: (b, h, 0, 0))],
        out_specs=pl.BlockSpec((TQ, gw), lambda b, h, qi: (b * nq + qi, h)),
        out_shape=jax.ShapeDtypeStruct((n_batch * seq, D_MODEL), F32),
        scratch_shapes=[pltpu.VMEM((rows, FOX_HD), F32), pltpu.VMEM((rows, 1), F32),
                        pltpu.VMEM((rows, 1), F32)],
        compiler_params=_params(("arbitrary", "arbitrary", "arbitrary")),
        name="fox_attn_prompt",
    )(q, kb, vb, c_t)


def _lane_cumsum(x):
    lane = lax.broadcasted_iota(jnp.int32, x.shape, 1)
    d = 1
    while d < x.shape[1]:
        x = x + jnp.where(lane >= d, pltpu.roll(x, d, 1), 0.0)
        d *= 2
    return x


def _fox_dec_kernel(pt_ref, q_ref, knew_ref, vnew_ref, lfnew_ref, *rest, n_pages, n_new):
    g_pages = PAGES_PER_STEP
    k_refs = rest[:g_pages]
    v_refs = rest[g_pages:2 * g_pages]
    lf_refs = rest[2 * g_pages:3 * g_pages]
    o_ref = rest[3 * g_pages]
    kbuf, vbuf, lfpad, carry_ref, m_ref, l_ref, acc_ref = rest[3 * g_pages + 1:]
    g = pl.program_id(1)
    kvw = FOX_KVH * FOX_HD

    @pl.when(g == 0)
    def _():
        _flash_init(m_ref, l_ref, acc_ref)
        carry_ref[...] = jnp.zeros(carry_ref.shape, F32)
        lfpad[...] = jnp.zeros(lfpad.shape, F32)

    for i in range(g_pages):
        kbuf[i * PAGE:(i + 1) * PAGE, :] = k_refs[i][...].astype(BF16)
        vbuf[i * PAGE:(i + 1) * PAGE, :] = v_refs[i][...].astype(BF16)
        lfpad[i, :, 0:FOX_HEADS] = lf_refs[i][...]

    lf_t = jnp.concatenate([lfpad[i].T[0:FOX_HEADS, :] for i in range(g_pages)], axis=0)
    cum = _lane_cumsum(lf_t)
    tot = jnp.broadcast_to(cum[:, PAGE - 1:PAGE], cum.shape)
    inc = tot
    d = FOX_HEADS
    while d < g_pages * FOX_HEADS:
        inc = inc + jnp.concatenate([jnp.zeros((d, PAGE), F32), inc[:-d, :]], axis=0)
        d *= 2
    carry = carry_ref[...]
    c_keys = cum + (inc - tot) + jnp.concatenate([carry] * g_pages, axis=0)
    carry_ref[...] = carry + inc[(g_pages - 1) * FOX_HEADS:, :]

    q = q_ref[0]
    row_kvh = (lax.broadcasted_iota(jnp.int32, (128, kvw), 0) % FOX_HEADS) // FOX_GROUP
    col_kvh = lax.broadcasted_iota(jnp.int32, (128, kvw), 1) // FOX_HD
    own = row_kvh == col_kvh
    qbd = jnp.where(own, jnp.concatenate([q] * FOX_KVH, axis=1), jnp.zeros((), BF16))
    bias = jnp.concatenate(
        [jnp.concatenate([c_keys[i * FOX_HEADS:(i + 1) * FOX_HEADS, :]] * (128 // FOX_HEADS), axis=0)
         for i in range(g_pages)], axis=1)
    _flash_update(_dot_nt(qbd, kbuf[...]) - bias, vbuf[...], m_ref, l_ref, acc_ref)

    @pl.when(g == n_pages // g_pages - 1)
    def _():
        c_new = carry_ref[...] + _lane_cumsum(lfnew_ref[0])
        bias_n = jnp.concatenate([c_new] * (128 // FOX_HEADS), axis=0)[:, 0:16]
        s = _dot_nt(qbd, knew_ref[0]) - bias_n
        tok = lax.broadcasted_iota(jnp.int32, s.shape, 0) // FOX_HEADS
        key = lax.broadcasted_iota(jnp.int32, s.shape, 1)
        s = jnp.where((key <= tok) & (key < n_new), s, NEG)
        _flash_update(s, vnew_ref[0], m_ref, l_ref, acc_ref)
        o = jnp.where(own, acc_ref[...] / l_ref[...], 0.0)
        o = o[:, 0:128] + o[:, 128:256] + o[:, 256:384] + o[:, 384:512]
        o_ref[0] = o[0:n_new * FOX_HEADS, :].reshape(n_new, FOX_HEADS, FOX_HD)


def _fox_attn_sample(page_table, q_s, knew, vnew, lfnew_t, cache_k, cache_v, cache_lf, layer, n_new):
    n_b, n_pages = page_table.shape
    g_pages = PAGES_PER_STEP
    pt = page_table.reshape(-1)
    kvw = FOX_KVH * FOX_HD

    def page_spec(width, i):
        return pl.BlockSpec((None, None, PAGE, width),
                            lambda b, g, pt_ref: (layer, pt_ref[b * n_pages + g * g_pages + i], 0, 0))

    per_b = lambda shape: pl.BlockSpec((1,) + shape, lambda b, g, pt_ref: (b, 0, 0))
    grid_spec = pltpu.PrefetchScalarGridSpec(
        num_scalar_prefetch=1,
        grid=(n_b, n_pages // g_pages),
        in_specs=[per_b((128, FOX_HD)), per_b((16, kvw)), per_b((16, kvw)), per_b((FOX_HEADS, 128))]
                 + [page_spec(kvw, i) for i in range(g_pages)]
                 + [page_spec(kvw, i) for i in range(g_pages)]
                 + [page_spec(FOX_HEADS, i) for i in range(g_pages)],
        out_specs=pl.BlockSpec((1, n_new, FOX_HEADS, FOX_HD), lambda b, g, pt_ref: (b, 0, 0, 0)),
        scratch_shapes=[pltpu.VMEM((g_pages * PAGE, kvw), BF16), pltpu.VMEM((g_pages * PAGE, kvw), BF16),
                        pltpu.VMEM((g_pages, PAGE, 128), F32), pltpu.VMEM((FOX_HEADS, 128), F32),
                        pltpu.VMEM((128, 1), F32), pltpu.VMEM((128, 1), F32), pltpu.VMEM((128, kvw), F32)],
    )
    return pl.pallas_call(
        functools.partial(_fox_dec_kernel, n_pages=n_pages, n_new=n_new),
        grid_spec=grid_spec,
        out_shape=jax.ShapeDtypeStruct((n_b, n_new, FOX_HEADS, FOX_HD), F32),
        compiler_params=_params(("arbitrary", "arbitrary")),
        name="fox_attn_sample",
    )(pt, q_s, knew, vnew, lfnew_t, *([cache_k] * g_pages), *([cache_v] * g_pages), *([cache_lf] * g_pages))


def _rms_kernel(x_ref, g_ref, h_ref):
    h_ref[...] = _rms(x_ref[...], g_ref[...])


def _rms_norm(x, g):
    m = x.shape[0]
    row = pl.BlockSpec((TM, D_MODEL), lambda i: (i, 0))
    return pl.pallas_call(
        _rms_kernel, grid=(m // TM,), in_specs=[row, _resident((1, D_MODEL))], out_specs=row,
        out_shape=jax.ShapeDtypeStruct((m, D_MODEL), F32), compiler_params=_params(("arbitrary",)),
        name="rms_norm",
    )(x, g)


def _rwkv_mix_kernel(h_ref, hp_ref, mu_ref, w_ref, o_ref):
    h = h_ref[...]
    xm = h + (hp_ref[...] - h) * mu_ref[0]
    o_ref[0] = _dot(xm.astype(BF16), w_ref[0])


def _rwkv_mix_proj(h, hp, mu4, w4):
    m = h.shape[0]
    row = pl.BlockSpec((TM, D_MODEL), lambda p, i: (i, 0))
    return pl.pallas_call(
        _rwkv_mix_kernel,
        grid=(4, m // TM),
        in_specs=[row, row, pl.BlockSpec((1, 1, D_MODEL), lambda p, i: (p, 0, 0)),
                  pl.BlockSpec((1, D_MODEL, D_MODEL), lambda p, i: (p, 0, 0))],
        out_specs=pl.BlockSpec((1, TM, D_MODEL), lambda p, i: (p, i, 0)),
        out_shape=jax.ShapeDtypeStruct((4, m, D_MODEL), F32),
        compiler_params=_params(("arbitrary", "arbitrary")),
        name="rwkv_mix_proj",
    )(h, hp, mu4, w4)


def _rwkv_post_kernel(h_ref, hp_ref, k_ref, muw_ref, mua_ref, w0_ref, w1_ref, w2_ref, a0_ref, a1_ref, a2_ref,
                      kk_ref, ka_ref, lw_ref, k4_ref, kr_ref, as_ref):
    h = h_ref[...]
    dh = hp_ref[...] - h
    xw = (h + dh * muw_ref[...]).astype(BF16)
    xa = (h + dh * mua_ref[...]).astype(BF16)
    wl = w0_ref[...] + _dot(jnp.tanh(_dot(xw, w1_ref[...])).astype(BF16), w2_ref[...])
    w_log = -_softplus(-wl) - 0.5
    lw_ref[...] = -jnp.exp(w_log)
    a = _sigmoid(a0_ref[...] + _dot(_dot(xa, a1_ref[...]).astype(BF16), a2_ref[...]))
    k = k_ref[0]
    k4_ref[...] = k * (1.0 + (a - 1.0) * ka_ref[...])
    kr_ref[...] = k * kk_ref[...]
    as_ref[...] = a


def _rwkv_post(h, hp, rkvg, muw, mua, w0, w1, w2, a0, a1, a2, kk, ka):
    m = h.shape[0]
    row = pl.BlockSpec((TM, D_MODEL), lambda i: (i, 0))
    vec = _resident((1, D_MODEL))
    return pl.pallas_call(
        _rwkv_post_kernel,
        grid=(m // TM,),
        in_specs=[row, row, pl.BlockSpec((1, TM, D_MODEL), lambda i: (1, i, 0)), vec, vec, vec,
                  _resident(w1.shape), _resident(w2.shape), vec, _resident(a1.shape), _resident(a2.shape),
                  vec, vec],
        out_specs=[row, row, row, row],
        out_shape=[jax.ShapeDtypeStruct((m, D_MODEL), F32)] * 4,
        compiler_params=_params(("arbitrary",)),
        name="rwkv_post",
    )(h, hp, rkvg, muw, mua, w0, w1, w2, a0, a1, a2, kk, ka)


def _dotf(a, b, dims=None):
    if dims is None:
        return jnp.dot(a, b, preferred_element_type=F32, precision=HI)
    return lax.dot_general(a, b, dims, preferred_element_type=F32, precision=HI)


def _rwkv_chunk(state, r, lw, k, v, kr, asig, rk, lnw, lnb):
    n = r.shape[0]
    kk = kr * lax.rsqrt(jnp.maximum(jnp.sum(kr * kr, axis=-1, keepdims=True), 1e-24))
    a = -kk
    b = kk * asig
    row = lax.broadcasted_iota(jnp.int32, (n, n), 0)
    col = lax.broadcasted_iota(jnp.int32, (n, n), 1)
    lower = row >= col
    strict = row > col
    cl = _dotf(lower.astype(F32), lw)
    p_in = jnp.exp(cl)
    p_inv = jnp.exp(-cl)
    rt = r * p_in
    at = a * jnp.exp(cl - lw)
    kt = k * p_inv
    bt = b * p_inv
    a_ab = jnp.where(strict, _dotf(at, bt, NT), 0.0)
    a_ak = jnp.where(strict, _dotf(at, kt, NT), 0.0)
    a_rb = jnp.where(lower, _dotf(rt, bt, NT), 0.0)
    a_rk = jnp.where(lower, _dotf(rt, kt, NT), 0.0)
    u = _dotf(at, state, NT) + _dotf(a_ak, v)
    nk = a_ab
    for lvl in range(int(math.log2(n))):
        u = u + _dotf(nk, u)
        if lvl + 1 < int(math.log2(n)):
            nk = _dotf(nk, nk)
    y = _dotf(rt, state, NT) + _dotf(a_rb, u) + _dotf(a_rk, v)
    new_state = (state + _dotf(u, bt, TN) + _dotf(v, kt, TN)) * p_in[n - 1:n, :]
    mean = jnp.mean(y, axis=-1, keepdims=True)
    var = jnp.mean(jnp.square(y - mean), axis=-1, keepdims=True)
    yn = (y - mean) * lax.rsqrt(var + RWKV_LNX_EPS) * lnw + lnb
    bonus = jnp.sum(r * k * rk, axis=-1, keepdims=True) * v
    return new_state, yn + bonus


def _rwkv_scan_kernel(r_ref, lw_ref, k_ref, v_ref, kr_ref, as_ref, s0_ref, rk_ref, lnw_ref, lnb_ref,
                      u_ref, sout_ref, state_ref, *, heads, n_chunks):
    c = pl.program_id(2)

    @pl.when(c == 0)
    def _():
        state_ref[...] = s0_ref[0]

    outs = []
    for h in range(heads):
        sl = slice(h * RWKV_HD, (h + 1) * RWKV_HD)
        new_state, u = _rwkv_chunk(state_ref[h], r_ref[:, sl], lw_ref[:, sl], k_ref[:, sl], v_ref[:, sl],
                                   kr_ref[:, sl], as_ref[:, sl], rk_ref[:, sl], lnw_ref[:, sl], lnb_ref[:, sl])
        state_ref[h] = new_state
        outs.append(u)
    u_ref[...] = jnp.concatenate(outs, axis=-1)

    @pl.when(c == n_chunks - 1)
    def _():
        sout_ref[0] = state_ref[...]


def _rwkv_scan(r, lw, k, v, kr, asig, s0, rk, lnw, lnb, n_seq, seq, chunk, heads):
    n_chunks = seq // chunk
    w = heads * RWKV_HD
    tok = pl.BlockSpec((chunk, w), lambda s, hg, c: (s * n_chunks + c, hg))
    vec = pl.BlockSpec((1, w), lambda s, hg, c: (0, hg))
    st = pl.BlockSpec((1, heads, RWKV_HD, RWKV_HD), lambda s, hg, c: (s, hg, 0, 0))
    return pl.pallas_call(
        functools.partial(_rwkv_scan_kernel, heads=heads, n_chunks=n_chunks),
        grid=(n_seq, RWKV_HEADS // heads, n_chunks),
        in_specs=[tok, tok, tok, tok, tok, tok, st, vec, vec, vec],
        out_specs=[tok, st],
        out_shape=[jax.ShapeDtypeStruct((n_seq * seq, D_MODEL), F32),
                   jax.ShapeDtypeStruct((n_seq, RWKV_HEADS, RWKV_HD, RWKV_HD), F32)],
        scratch_shapes=[pltpu.VMEM((heads, RWKV_HD, RWKV_HD), F32)],
        compiler_params=_params(("arbitrary", "arbitrary", "arbitrary")),
        name="rwkv_scan",
    )(r, lw, k, v, kr, asig, s0, rk, lnw, lnb)


def _out_proj_kernel(u_ref, g_ref, x_ref, w_ref, gp_ref, o_ref):
    g = g_ref[...]
    a = (u_ref[...] * (g * _sigmoid(g))).astype(BF16)
    o_ref[...] = x_ref[...] + _rms(_dot(a, w_ref[...]), gp_ref[...])


def _out_proj(u, gate, x, w_o, gpost):
    m = x.shape[0]
    row = pl.BlockSpec((TM, D_MODEL), lambda i: (i, 0))
    return pl.pallas_call(
        _out_proj_kernel, grid=(m // TM,),
        in_specs=[row, row, row, _resident(w_o.shape), _resident((1, D_MODEL))], out_specs=row,
        out_shape=jax.ShapeDtypeStruct((m, D_MODEL), F32), compiler_params=_params(("arbitrary",)),
        name="out_proj",
    )(u, gate, x, w_o, gpost)


def _mla_layer(x, dims, page_table, cache_ckv, cache_kpe, layer, tables, gpre, w_in, q_norm, kv_norm, w_uq, w_ukv):
    n_b, seq, n_db, n_new = dims
    mp = n_b * seq
    w_main = jnp.concatenate(
        [w_in[:, 0:1024], w_in[:, 1024:1088],
         -w_in[:, 1056:1088], w_in[:, 1024:1056], w_in[:, 1088:]], axis=1).astype(BF16)
    uq3 = w_uq.reshape(MLA_LORA, MLA_HEADS, MLA_NOPE + MLA_ROPE)
    wuqn = uq3[:, :, 0:MLA_NOPE].reshape(MLA_LORA, -1).astype(BF16)
    pe = uq3[:, :, MLA_NOPE:]
    wuqp = pe.reshape(MLA_LORA, -1).astype(BF16)
    wuqr = jnp.concatenate([-pe[:, :, 32:], pe[:, :, :32]], axis=-1).reshape(MLA_LORA, -1).astype(BF16)
    wk = jnp.transpose(w_ukv[:, :, 0:MLA_NOPE], (1, 2, 0)).astype(BF16)
    wv = jnp.transpose(w_ukv[:, :, MLA_NOPE:], (1, 0, 2)).astype(BF16)
    gate, ckv, kpe, kcat, qfull = _mla_proj(x, gpre, w_main, q_norm, kv_norm, wuqn, wuqp, wuqr, wk, *tables)
    u_p = _mla_attn_prompt(qfull, kcat, wv, n_b, seq)
    q_s = jnp.transpose(qfull[:, mp:, :], (1, 0, 2)).reshape(n_db, n_new * MLA_HEADS, MLA_KW)
    q_s = jnp.pad(q_s, ((0, 0), (0, 128 - n_new * MLA_HEADS), (0, 0)))
    knew = jnp.pad(kcat[mp:].reshape(n_db, n_new, MLA_KW), ((0, 0), (0, 16 - n_new), (0, 0)))
    o_lat = _mla_attn_sample(page_table, q_s, knew, cache_ckv, cache_kpe, layer, n_new)
    u_s = _mla_up(o_lat.reshape(n_db * n_new, MLA_HEADS * MLA_LORA), wv)
    return jnp.concatenate([u_p, u_s], axis=0), gate, ckv, kpe


def _fox_layer(x, dims, page_table, cache_k, cache_v, cache_lf, layer, gpre, w_in, b_f, q_norm, k_norm):
    n_b, seq, n_db, n_new = dims
    mp = n_b * seq
    kvw = FOX_KVH * FOX_HD
    nq = FOX_HEADS * FOX_HD
    w_main = jnp.concatenate(
        [w_in[:, 0:nq + 2 * kvw], w_in[:, nq + 2 * kvw:nq + 2 * kvw + FOX_HEADS],
         jnp.zeros((D_MODEL, 128 - FOX_HEADS), F32), w_in[:, nq + 2 * kvw + FOX_HEADS:]], axis=1).astype(BF16)
    bf = jnp.pad(b_f, (0, 128 - FOX_HEADS)).reshape(1, 128)
    q, k, v, kb, vb, lf, c, gate = _fox_proj(x, gpre, w_main, bf, q_norm.reshape(1, -1), k_norm.reshape(1, -1),
                                             seq // TM)
    c_t = jnp.transpose(c[:mp].reshape(n_b, seq, FOX_HEADS), (0, 2, 1)).reshape(n_b, FOX_KVH, FOX_GROUP, seq)
    u_p = _fox_attn_prompt(q, kb, vb, c_t, n_b, seq)
    q_s = jnp.pad(q[mp:].reshape(n_db, n_new * FOX_HEADS, FOX_HD), ((0, 0), (0, 128 - n_new * FOX_HEADS), (0, 0)))
    knew = jnp.pad(kb[mp:].reshape(n_db, n_new, kvw), ((0, 0), (0, 16 - n_new), (0, 0)))
    vnew = jnp.pad(vb[mp:].reshape(n_db, n_new, kvw), ((0, 0), (0, 16 - n_new), (0, 0)))
    lfnew_t = jnp.pad(jnp.transpose(lf[mp:].reshape(n_db, n_new, FOX_HEADS), (0, 2, 1)),
                      ((0, 0), (0, 0), (0, 128 - n_new)))
    n_pool = cache_k.shape[1]
    o_s = _fox_attn_sample(page_table, q_s, knew, vnew, lfnew_t,
                           cache_k.reshape(cache_k.shape[0], n_pool, PAGE, kvw),
                           cache_v.reshape(cache_v.shape[0], n_pool, PAGE, kvw), cache_lf, layer, n_new)
    u = jnp.concatenate([u_p, o_s.reshape(n_db * n_new, D_MODEL)], axis=0)
    return u, gate, k, v, lf


def _rwkv_layer(x, dims, shift0, s0, gpre, mu, w_rkvg, w0, w1, w2, a0, a1, a2, k_k, k_a, r_k, lnx_w, lnx_b):
    n_b, seq, n_db, n_new = dims
    mp = n_b * seq
    vec = lambda a: a.reshape(1, D_MODEL)
    h = _rms_norm(x, gpre)
    h_p = h[:mp].reshape(n_b, seq, D_MODEL)
    h_s = h[mp:].reshape(n_db, n_new, D_MODEL)
    hp = jnp.concatenate(
        [jnp.concatenate([jnp.zeros((n_b, 1, D_MODEL), F32), h_p[:, :-1]], axis=1).reshape(mp, D_MODEL),
         jnp.concatenate([shift0[:, None, :], h_s[:, :-1]], axis=1).reshape(n_db * n_new, D_MODEL)], axis=0)
    rkvg = _rwkv_mix_proj(h, hp, mu[0:4].reshape(4, 1, D_MODEL), w_rkvg.astype(BF16))
    lora = w1.shape[1]
    w1p = jnp.pad(w1, ((0, 0), (0, RWKV_LORA_PAD - lora))).astype(BF16)
    w2p = jnp.pad(w2, ((0, RWKV_LORA_PAD - lora), (0, 0))).astype(BF16)
    a1p = jnp.pad(a1, ((0, 0), (0, RWKV_LORA_PAD - lora))).astype(BF16)
    a2p = jnp.pad(a2, ((0, RWKV_LORA_PAD - lora), (0, 0))).astype(BF16)
    lw, k4, kr, asig = _rwkv_post(h, hp, rkvg, vec(mu[4]), vec(mu[5]), vec(w0), w1p, w2p, vec(a0), a1p, a2p,
                                  vec(k_k), vec(k_a))
    r = rkvg[0]
    v = rkvg[2]
    params = (vec(r_k), vec(lnx_w), vec(lnx_b))
    zeros_state = jnp.zeros((n_b, RWKV_HEADS, RWKV_HD, RWKV_HD), F32)
    u_p, s_p = _rwkv_scan(r[:mp], lw[:mp], k4[:mp], v[:mp], kr[:mp], asig[:mp], zeros_state, *params,
                          n_seq=n_b, seq=seq, chunk=64, heads=4)
    pad_s = lambda a: jnp.pad(a[mp:].reshape(n_db, n_new, D_MODEL),
                              ((0, 0), (0, 8 - n_new), (0, 0))).reshape(n_db * 8, D_MODEL)
    u_s, s_s = _rwkv_scan(pad_s(r), pad_s(lw), pad_s(k4), pad_s(v), pad_s(kr), pad_s(asig), s0, *params,
                          n_seq=n_db, seq=8, chunk=8, heads=4)
    u_s = u_s.reshape(n_db, 8, D_MODEL)[:, :n_new].reshape(n_db * n_new, D_MODEL)
    u = jnp.concatenate([u_p, u_s], axis=0)
    return u, rkvg[3], s_p, s_s, h_p[:, -1], h_s[:, -1]


def _rope_tables(n_b, seq, n_db, n_new, n_past):
    half = MLA_ROPE // 2
    inv = ROPE_THETA ** (-jnp.arange(half, dtype=F32) / half)
    pos = jnp.concatenate([jnp.tile(jnp.arange(seq, dtype=F32), n_b),
                           jnp.tile(n_past + jnp.arange(n_new, dtype=F32), n_db)])
    ang = pos[:, None] * inv[None, :]
    return jnp.tile(jnp.cos(ang), (1, 4)), jnp.tile(jnp.sin(ang), (1, 4))


def kernel(x_prompt, x_sample, cache_mla_ckv, cache_mla_kpe, cache_fox_k, cache_fox_v, cache_fox_logf,
           state_rwkv_wkv, state_rwkv_shift, page_table, norm_pre, norm_post,
           mla_w_in, mla_q_norm, mla_kv_norm, mla_w_uq, mla_w_ukv, mla_w_o,
           fox_w_in, fox_b_f, fox_q_norm, fox_k_norm, fox_w_o,
           rwkv_mu, rwkv_w_rkvg, rwkv_w0, rwkv_w1, rwkv_w2, rwkv_a0, rwkv_a1, rwkv_a2,
           rwkv_k_k, rwkv_k_a, rwkv_r_k, rwkv_lnx_w, rwkv_lnx_b, rwkv_w_o):
    n_b, seq, _ = x_prompt.shape
    n_db, n_new, _ = x_sample.shape
    n_pages = page_table.shape[1]
    mp, ms = n_b * seq, n_db * n_new
    assert seq % TQ == 0 and seq % TM == 0 and ms % TM == 0 and n_pages % PAGES_PER_STEP == 0
    assert n_new <= 8 and n_new * FOX_HEADS <= 128
    dims = (n_b, seq, n_db, n_new)
    depth = norm_pre.shape[0]
    x = jnp.concatenate([x_prompt.reshape(mp, D_MODEL), x_sample.reshape(ms, D_MODEL)], axis=0)
    tables = _rope_tables(n_b, seq, n_db, n_new, n_pages * PAGE)
    row = lambda a: a.reshape(1, -1)
    outs = {name: [] for name in ("ckv", "kpe", "k", "v", "lf", "wkv_p", "wkv_s", "sh_p", "sh_s")}
    for i in range(depth):
        j, kind = divmod(i, 3)
        gpre = row(norm_pre[i])
        if kind == 0:
            u, gate, ckv, kpe = _mla_layer(x, dims, page_table, cache_mla_ckv, cache_mla_kpe, j, tables, gpre,
                                           mla_w_in[j], row(mla_q_norm[j]), row(mla_kv_norm[j]),
                                           mla_w_uq[j], mla_w_ukv[j])
            w_o = mla_w_o[j]
            outs["ckv"].append(ckv)
            outs["kpe"].append(kpe)
        elif kind == 1:
            u, gate, k, v, lf = _fox_layer(x, dims, page_table, cache_fox_k, cache_fox_v, cache_fox_logf, j, gpre,
                                           fox_w_in[j], fox_b_f[j], fox_q_norm[j], fox_k_norm[j])
            w_o = fox_w_o[j]
            outs["k"].append(k)
            outs["v"].append(v)
            outs["lf"].append(lf)
        else:
            u, gate, s_p, s_s, sh_p, sh_s = _rwkv_layer(
                x, dims, state_rwkv_shift[j], state_rwkv_wkv[j], gpre, rwkv_mu[j], rwkv_w_rkvg[j],
                rwkv_w0[j], rwkv_w1[j], rwkv_w2[j], rwkv_a0[j], rwkv_a1[j], rwkv_a2[j],
                rwkv_k_k[j], rwkv_k_a[j], rwkv_r_k[j].reshape(-1), rwkv_lnx_w[j], rwkv_lnx_b[j])
            w_o = rwkv_w_o[j]
            outs["wkv_p"].append(s_p)
            outs["wkv_s"].append(s_s)
            outs["sh_p"].append(sh_p)
            outs["sh_s"].append(sh_s)
        x = _out_proj(u, gate, x, w_o.astype(BF16), row(norm_post[i]))

    def split(name, *tail):
        a = jnp.stack(outs[name])
        return (a[:, :mp].reshape((a.shape[0], n_b, seq) + tail),
                a[:, mp:].reshape((a.shape[0], n_db, n_new) + tail))

    p_ckv, s_ckv = split("ckv", MLA_LORA)
    p_kpe, s_kpe = split("kpe", MLA_ROPE)
    p_k, s_k = split("k", FOX_KVH, FOX_HD)
    p_v, s_v = split("v", FOX_KVH, FOX_HD)
    p_lf, s_lf = split("lf", FOX_HEADS)
    return (x[:mp].reshape(n_b, seq, D_MODEL), x[mp:].reshape(n_db, n_new, D_MODEL),
            p_ckv, p_kpe, p_k, p_v, p_lf, jnp.stack(outs["wkv_p"]), jnp.stack(outs["sh_p"]),
            s_ckv, s_kpe, s_k, s_v, s_lf, jnp.stack(outs["wkv_s"]), jnp.stack(outs["sh_s"]))
```

```python
import functools
import math

import jax
import jax.numpy as jnp
from jax import lax
from jax.experimental import pallas as pl
from jax.experimental.pallas import tpu as pltpu

F32 = jnp.float32
BF16 = jnp.bfloat16

D_MODEL = 2048
NORM_EPS = 1e-6
PAGE = 128

MLA_NOPE = 128
MLA_ROPE = 64
MLA_V = 128
MLA_HEADS = 16
MLA_HEAD_GROUP = 4
MLA_LORA = 512
MLA_SCALE = (MLA_NOPE + MLA_ROPE) ** -0.5
MLA_KW = MLA_LORA + 128
ROPE_THETA = 10000.0

FOX_HD = 128
FOX_HEADS = 16
FOX_KVH = 4
FOX_GROUP = 4
FOX_SCALE = FOX_HD ** -0.5

RWKV_HD = 64
RWKV_HEADS = 32
RWKV_LORA_PAD = 128
RWKV_CHUNK = 64
RWKV_SCAN_HEADS = 8
RWKV_LNX_EPS = 64e-5

TM = 256
TQ = 256
PAGES_PER_STEP = 16
NEG = -1e30
VMEM_LIMIT = 56 * 1024 * 1024
HI = lax.Precision.HIGHEST

NT = (((1,), (1,)), ((), ()))
TN = (((0,), (0,)), ((), ()))


def _params(sem):
    return pltpu.CompilerParams(dimension_semantics=sem, vmem_limit_bytes=VMEM_LIMIT)


def _resident(shape):
    nd = len(shape)
    return pl.BlockSpec(shape, lambda *_: (0,) * nd, pipeline_mode=pl.Buffered(1))


def _dot(a, b):
    return jnp.dot(a, b, preferred_element_type=F32)


def _dot_nt(a, b):
    return lax.dot_general(a, b, NT, preferred_element_type=F32)


def _rms(x, g):
    return x * lax.rsqrt(jnp.mean(x * x, axis=-1, keepdims=True) + NORM_EPS) * g


def _sigmoid(x):
    return 1.0 / (1.0 + jnp.exp(-x))


def _softplus(x):
    return jnp.maximum(x, 0.0) + jnp.log(1.0 + jnp.exp(-jnp.abs(x)))


def _mla_proj_kernel(x_ref, gpre_ref, w_ref, qn_ref, kvn_ref, wuqn_ref, wuqp_ref, wuqr_ref, wk_ref,
                     cos_ref, sin_ref, gate_ref, ckv_ref, kpe_ref, kcat_ref, qfull_ref):
    hb = _rms(x_ref[...], gpre_ref[...]).astype(BF16)
    cq = _rms(_dot(hb, w_ref[:, 0:512]), qn_ref[...]).astype(BF16)
    ckv = _rms(_dot(hb, w_ref[:, 512:1024]), kvn_ref[...])
    z = _dot(hb, w_ref[:, 1024:1152])
    cos_t = cos_ref[...]
    sin_t = sin_ref[...]
    lane = lax.broadcasted_iota(jnp.int32, z.shape, 1)
    kpe = jnp.where(lane < MLA_ROPE, z * cos_t + pltpu.roll(z, 64, 1) * sin_t, 0.0)
    gate_ref[...] = _dot(hb, w_ref[:, 1152:3200])
    ckv_ref[...] = ckv
    kpe_ref[...] = kpe[:, 0:MLA_ROPE]
    kcat_ref[:, 0:512] = ckv.astype(BF16)
    kcat_ref[:, 512:640] = kpe.astype(BF16)
    qn = _dot(cq, wuqn_ref[...]).astype(BF16)
    cos8 = jnp.concatenate([cos_t] * 8, axis=1)
    sin8 = jnp.concatenate([sin_t] * 8, axis=1)
    qpe = (_dot(cq, wuqp_ref[...]) * cos8 + _dot(cq, wuqr_ref[...]) * sin8) * MLA_SCALE
    for h in range(MLA_HEADS):
        ql = _dot(qn[:, h * 128:(h + 1) * 128], wk_ref[h]) * MLA_SCALE
        qfull_ref[h, :, 0:512] = ql.astype(BF16)
        blk = qpe[:, (h // 2) * 128:(h // 2 + 1) * 128]
        if h % 2:
            blk = pltpu.roll(blk, 64, 1)
        qfull_ref[h, :, 512:640] = jnp.where(lane < MLA_ROPE, blk, 0.0).astype(BF16)


def _mla_proj(x, gpre, w_main, qn, kvn, wuqn, wuqp, wuqr, wk, cos_t, sin_t):
    m = x.shape[0]
    row = lambda w: pl.BlockSpec((TM, w), lambda i: (i, 0))
    return pl.pallas_call(
        _mla_proj_kernel,
        grid=(m // TM,),
        in_specs=[row(D_MODEL), _resident((1, D_MODEL)), _resident(w_main.shape), _resident((1, 512)),
                  _resident((1, 512)), _resident(wuqn.shape), _resident(wuqp.shape), _resident(wuqr.shape),
                  _resident(wk.shape), row(128), row(128)],
        out_specs=[row(D_MODEL), row(512), row(MLA_ROPE), row(MLA_KW),
                   pl.BlockSpec((MLA_HEADS, TM, MLA_KW), lambda i: (0, i, 0))],
        out_shape=[jax.ShapeDtypeStruct((m, D_MODEL), F32), jax.ShapeDtypeStruct((m, 512), F32),
                   jax.ShapeDtypeStruct((m, MLA_ROPE), F32), jax.ShapeDtypeStruct((m, MLA_KW), BF16),
                   jax.ShapeDtypeStruct((MLA_HEADS, m, MLA_KW), BF16)],
        compiler_params=_params(("arbitrary",)),
        name="mla_proj",
    )(x, gpre, w_main, qn, kvn, wuqn, wuqp, wuqr, wk, cos_t, sin_t)


def _flash_update(scores, values, m_ref, l_ref, acc_ref):
    probs, alphas = {}, {}
    for c, s in scores.items():
        m_prev = m_ref[c]
        m_new = jnp.maximum(m_prev, jnp.max(s, axis=-1, keepdims=True))
        alphas[c] = jnp.exp(m_prev - m_new)
        p = jnp.exp(s - m_new)
        l_ref[c] = alphas[c] * l_ref[c] + jnp.sum(p, axis=-1, keepdims=True)
        m_ref[c] = m_new
        probs[c] = p.astype(BF16)
    for c in scores:
        acc_ref[c] = alphas[c] * acc_ref[c] + _dot(probs[c], values[c])


def _flash_init(m_ref, l_ref, acc_ref):
    m_ref[...] = jnp.full(m_ref.shape, NEG, F32)
    l_ref[...] = jnp.zeros(l_ref.shape, F32)
    acc_ref[...] = jnp.zeros(acc_ref.shape, F32)


def _flash_merge(m_ref, l_ref, acc_ref):
    m0, m1 = m_ref[0], m_ref[1]
    m = jnp.maximum(m0, m1)
    a0 = jnp.exp(m0 - m)
    a1 = jnp.exp(m1 - m)
    return (acc_ref[0] * a0 + acc_ref[1] * a1) / (l_ref[0] * a0 + l_ref[1] * a1)


def _flash_scratch(chains, rows, width):
    return [pltpu.VMEM((chains, rows, width), F32), pltpu.VMEM((chains, rows, 1), F32),
            pltpu.VMEM((chains, rows, 1), F32)]


def _mla_attn_prompt_kernel(q_ref, k_ref, wv_ref, o_ref, acc_ref, m_ref, l_ref):
    qi = pl.program_id(1)
    half = MLA_HEAD_GROUP // 2
    rows = half * TQ
    qs = [q_ref[c * half:(c + 1) * half].reshape(rows, MLA_KW) for c in range(2)]
    _flash_init(m_ref, l_ref, acc_ref)

    def block(kb, masked):
        k = k_ref[pl.ds(pl.multiple_of(kb * TQ, TQ), TQ), :]
        scores = {c: _dot_nt(qs[c], k) for c in range(2)}
        if masked:
            tok = lax.broadcasted_iota(jnp.int32, (rows, TQ), 0) & (TQ - 1)
            key = lax.broadcasted_iota(jnp.int32, (rows, TQ), 1)
            scores = {c: jnp.where(key <= tok, s, NEG) for c, s in scores.items()}
        v = k[:, 0:MLA_LORA]
        _flash_update(scores, {0: v, 1: v}, m_ref, l_ref, acc_ref)

    def body(kb, carry):
        block(kb, False)
        return carry

    lax.fori_loop(0, qi, body, 0)
    block(qi, True)
    for c in range(2):
        o = (acc_ref[c] / l_ref[c]).astype(BF16)
        for hh in range(half):
            h = c * half + hh
            o_ref[:, h * MLA_V:(h + 1) * MLA_V] = _dot(o[hh * TQ:(hh + 1) * TQ, :], wv_ref[h])


def _mla_attn_prompt(qfull, kcat, wv, n_batch, seq):
    nq = seq // TQ
    hg = MLA_HEAD_GROUP
    rows = hg // 2 * TQ
    return pl.pallas_call(
        _mla_attn_prompt_kernel,
        grid=(n_batch, nq, MLA_HEADS // hg),
        in_specs=[pl.BlockSpec((hg, TQ, MLA_KW), lambda b, q, g: (g, b * nq + q, 0)),
                  pl.BlockSpec((seq, MLA_KW), lambda b, q, g: (b, 0)),
                  pl.BlockSpec((hg, MLA_LORA, MLA_V), lambda b, q, g: (g, 0, 0))],
        out_specs=pl.BlockSpec((TQ, hg * MLA_V), lambda b, q, g: (b * nq + q, g)),
        out_shape=jax.ShapeDtypeStruct((n_batch * seq, D_MODEL), F32),
        scratch_shapes=_flash_scratch(2, rows, MLA_LORA),
        compiler_params=_params(("arbitrary", "arbitrary", "arbitrary")),
        name="mla_attn_prompt",
    )(qfull, kcat, wv)


def _mla_dec_kernel(pt_ref, q_ref, knew_ref, *rest, n_pages, n_new):
    g_pages = PAGES_PER_STEP
    ckv_refs = rest[:g_pages]
    kpe_refs = rest[g_pages:2 * g_pages]
    o_ref = rest[2 * g_pages]
    kbuf, kpbuf, acc_ref, m_ref, l_ref = rest[2 * g_pages + 1:]
    g = pl.program_id(1)
    half = g_pages * PAGE // 2

    @pl.when(g == 0)
    def _():
        _flash_init(m_ref, l_ref, acc_ref)

    for i in range(g_pages):
        kbuf[i * PAGE:(i + 1) * PAGE, :] = ckv_refs[i][...].astype(BF16)
        kpbuf[:, i * PAGE:(i + 1) * PAGE] = kpe_refs[i][...].astype(BF16)
    q = q_ref[0]
    q_lat = q[:, 0:MLA_LORA]
    q_pe = q[:, MLA_LORA:MLA_LORA + MLA_ROPE]
    keys = {c: kbuf[c * half:(c + 1) * half, :] for c in range(2)}
    scores = {c: _dot_nt(q_lat, k) + _dot(q_pe, kpbuf[:, c * half:(c + 1) * half]) for c, k in keys.items()}
    _flash_update(scores, keys, m_ref, l_ref, acc_ref)

    @pl.when(g == n_pages // g_pages - 1)
    def _():
        kn = knew_ref[0]
        s = _dot_nt(q, kn)
        tok = lax.broadcasted_iota(jnp.int32, s.shape, 0) // MLA_HEADS
        key = lax.broadcasted_iota(jnp.int32, s.shape, 1)
        s = jnp.where((key <= tok) & (key < n_new), s, NEG)
        _flash_update({0: s}, {0: kn[:, 0:MLA_LORA]}, m_ref, l_ref, acc_ref)
        o = _flash_merge(m_ref, l_ref, acc_ref)
        o_ref[0] = o[0:n_new * MLA_HEADS, :].reshape(n_new, MLA_HEADS, MLA_LORA)


def _mla_attn_sample(page_table, q_s, knew, cache_ckv, cache_kpe_t, layer, n_new):
    n_b, n_pages = page_table.shape
    g_pages = PAGES_PER_STEP
    pt = page_table.reshape(-1)

    def page_spec(rows, width, i):
        return pl.BlockSpec((None, None, rows, width),
                            lambda b, g, pt_ref: (layer, pt_ref[b * n_pages + g * g_pages + i], 0, 0))

    grid_spec = pltpu.PrefetchScalarGridSpec(
        num_scalar_prefetch=1,
        grid=(n_b, n_pages // g_pages),
        in_specs=[pl.BlockSpec((1, 128, MLA_KW), lambda b, g, pt_ref: (b, 0, 0)),
                  pl.BlockSpec((1, 16, MLA_KW), lambda b, g, pt_ref: (b, 0, 0))]
                 + [page_spec(PAGE, MLA_LORA, i) for i in range(g_pages)]
                 + [page_spec(MLA_ROPE, PAGE, i) for i in range(g_pages)],
        out_specs=pl.BlockSpec((1, n_new, MLA_HEADS, MLA_LORA), lambda b, g, pt_ref: (b, 0, 0, 0)),
        scratch_shapes=[pltpu.VMEM((g_pages * PAGE, MLA_LORA), BF16),
                        pltpu.VMEM((MLA_ROPE, g_pages * PAGE), BF16)] + _flash_scratch(2, 128, MLA_LORA),
    )
    return pl.pallas_call(
        functools.partial(_mla_dec_kernel, n_pages=n_pages, n_new=n_new),
        grid_spec=grid_spec,
        out_shape=jax.ShapeDtypeStruct((n_b, n_new, MLA_HEADS, MLA_LORA), F32),
        compiler_params=_params(("arbitrary", "arbitrary")),
        name="mla_attn_sample",
    )(pt, q_s, knew, *([cache_ckv] * g_pages), *([cache_kpe_t] * g_pages))


def _mla_up_kernel(o_ref, wv_ref, u_ref):
    for h in range(MLA_HEADS):
        oh = o_ref[:, h * MLA_LORA:(h + 1) * MLA_LORA].astype(BF16)
        u_ref[:, h * MLA_V:(h + 1) * MLA_V] = _dot(oh, wv_ref[h])


def _mla_up(o_lat, wv):
    m = o_lat.shape[0]
    return pl.pallas_call(
        _mla_up_kernel,
        grid=(m // TM,),
        in_specs=[pl.BlockSpec((TM, MLA_HEADS * MLA_LORA), lambda i: (i, 0)), _resident(wv.shape)],
        out_specs=pl.BlockSpec((TM, D_MODEL), lambda i: (i, 0)),
        out_shape=jax.ShapeDtypeStruct((m, D_MODEL), F32),
        compiler_params=_params(("arbitrary",)),
        name="mla_up",
    )(o_lat, wv)


def _head_rms(x, g, scale):
    return x * (lax.rsqrt(jnp.mean(x * x, axis=-1, keepdims=True) + NORM_EPS) * scale) * g


def _fox_proj_kernel(x_ref, gpre_ref, w_ref, bf_ref, qn_ref, kn_ref,
                     q_ref, k_ref, v_ref, kb_ref, vb_ref, lf_ref, c_ref, gate_ref, carry_ref, *, tiles_per_seq):
    i = pl.program_id(0)
    hb = _rms(x_ref[...], gpre_ref[...]).astype(BF16)
    q = _dot(hb, w_ref[:, 0:2048])
    for h in range(FOX_HEADS):
        sl = slice(h * FOX_HD, (h + 1) * FOX_HD)
        q_ref[:, sl] = _head_rms(q[:, sl], qn_ref[...], FOX_SCALE).astype(BF16)
    k = _dot(hb, w_ref[:, 2048:2560])
    for h in range(FOX_KVH):
        sl = slice(h * FOX_HD, (h + 1) * FOX_HD)
        kh = _head_rms(k[:, sl], kn_ref[...], 1.0)
        k_ref[:, sl] = kh
        kb_ref[:, sl] = kh.astype(BF16)
    v = _dot(hb, w_ref[:, 2560:3072])
    v_ref[...] = v
    vb_ref[...] = v.astype(BF16)
    fl = _dot(hb, w_ref[:, 3072:3200]) + bf_ref[...]
    lf = -_softplus(-fl)
    lf_ref[...] = lf[:, 0:FOX_HEADS]

    @pl.when(i % tiles_per_seq == 0)
    def _():
        carry_ref[...] = jnp.zeros(carry_ref.shape, F32)

    r = lax.broadcasted_iota(jnp.int32, (TM, TM), 0)
    c = lax.broadcasted_iota(jnp.int32, (TM, TM), 1)
    tri = (r >= c).astype(F32)
    cum = jnp.dot(tri, lf, preferred_element_type=F32, precision=HI) + carry_ref[...]
    c_ref[...] = cum[:, 0:FOX_HEADS]
    carry_ref[...] = cum[TM - 1:TM, :]
    gate_ref[...] = _dot(hb, w_ref[:, 3200:5248])


def _fox_proj(x, gpre, w_main, bf, qn, kn, tiles_per_seq):
    m = x.shape[0]
    row = lambda w: pl.BlockSpec((TM, w), lambda i: (i, 0))
    kvw = FOX_KVH * FOX_HD
    return pl.pallas_call(
        functools.partial(_fox_proj_kernel, tiles_per_seq=tiles_per_seq),
        grid=(m // TM,),
        in_specs=[row(D_MODEL), _resident((1, D_MODEL)), _resident(w_main.shape), _resident((1, 128)),
                  _resident((1, FOX_HD)), _resident((1, FOX_HD))],
        out_specs=[row(D_MODEL), row(kvw), row(kvw), row(kvw), row(kvw), row(FOX_HEADS), row(FOX_HEADS),
                   row(D_MODEL)],
        out_shape=[jax.ShapeDtypeStruct((m, D_MODEL), BF16), jax.ShapeDtypeStruct((m, kvw), F32),
                   jax.ShapeDtypeStruct((m, kvw), F32), jax.ShapeDtypeStruct((m, kvw), BF16),
                   jax.ShapeDtypeStruct((m, kvw), BF16), jax.ShapeDtypeStruct((m, FOX_HEADS), F32),
                   jax.ShapeDtypeStruct((m, FOX_HEADS), F32), jax.ShapeDtypeStruct((m, D_MODEL), F32)],
        scratch_shapes=[pltpu.VMEM((1, 128), F32)],
        compiler_params=_params(("arbitrary",)),
        name="fox_proj",
    )(x, gpre, w_main, bf, qn, kn)


def _fox_attn_prompt_kernel(q_ref, k_ref, v_ref, c_ref, o_ref, acc_ref, m_ref, l_ref):
    qi = pl.program_id(2)
    half = FOX_GROUP // 2
    rows = half * TQ
    qs = [jnp.concatenate([q_ref[:, g * FOX_HD:(g + 1) * FOX_HD] for g in range(c * half, (c + 1) * half)], axis=0)
          for c in range(2)]
    _flash_init(m_ref, l_ref, acc_ref)

    def block(kb, masked):
        start = pl.multiple_of(kb * TQ, TQ)
        k = k_ref[pl.ds(start, TQ), :]
        v = v_ref[pl.ds(start, TQ), :]
        cb = c_ref[0, 0, :, pl.ds(start, TQ)]
        scores = {c: _dot_nt(qs[c], k) for c in range(2)}
        scores = {c: (s.reshape(half, TQ, TQ) - cb[c * half:(c + 1) * half, None, :]).reshape(rows, TQ)
                  for c, s in scores.items()}
        if masked:
            tok = lax.broadcasted_iota(jnp.int32, (rows, TQ), 0) & (TQ - 1)
            key = lax.broadcasted_iota(jnp.int32, (rows, TQ), 1)
            scores = {c: jnp.where(key <= tok, s, NEG) for c, s in scores.items()}
        _flash_update(scores, {0: v, 1: v}, m_ref, l_ref, acc_ref)

    def body(kb, carry):
        block(kb, False)
        return carry

    lax.fori_loop(0, qi, body, 0)
    block(qi, True)
    for c in range(2):
        o = acc_ref[c] / l_ref[c]
        for gg in range(half):
            g = c * half + gg
            o_ref[:, g * FOX_HD:(g + 1) * FOX_HD] = o[gg * TQ:(gg + 1) * TQ, :]


def _fox_attn_prompt(q, kb, vb, c_t, n_batch, seq):
    nq = seq // TQ
    rows = FOX_GROUP // 2 * TQ
    gw = FOX_GROUP * FOX_HD
    return pl.pallas_call(
        _fox_attn_prompt_kernel,
        grid=(n_batch, FOX_KVH, nq),
        in_specs=[pl.BlockSpec((TQ, gw), lambda b, h, qi: (b * nq + qi, h)),
                  pl.BlockSpec((seq, FOX_HD), lambda b, h, qi: (b, h)),
                  pl.BlockSpec((seq, FOX_HD), lambda b, h, qi: (b, h)),
                  pl.BlockSpec((1, 1, FOX_GROUP, seq), lambda b, h, qi: (b, h, 0, 0))],
        out_specs=pl.BlockSpec((TQ, gw), lambda b, h, qi: (b * nq + qi, h)),
        out_shape=jax.ShapeDtypeStruct((n_batch * seq, D_MODEL), F32),
        scratch_shapes=_flash_scratch(2, rows, FOX_HD),
        compiler_params=_params(("arbitrary", "arbitrary", "arbitrary")),
        name="fox_attn_prompt",
    )(q, kb, vb, c_t)


def _lane_cumsum(x):
    lane = lax.broadcasted_iota(jnp.int32, x.shape, 1)
    d = 1
    while d < x.shape[1]:
        x = x + jnp.where(lane >= d, pltpu.roll(x, d, 1), 0.0)
        d *= 2
    return x


def _fox_dec_kernel(pt_ref, q_ref, knew_ref, vnew_ref, lfnew_ref, *rest, n_pages, n_new):
    g_pages = PAGES_PER_STEP
    k_refs = rest[:g_pages]
    v_refs = rest[g_pages:2 * g_pages]
    lf_refs = rest[2 * g_pages:3 * g_pages]
    o_ref = rest[3 * g_pages]
    kbuf, vbuf, carry_ref, acc_ref, m_ref, l_ref = rest[3 * g_pages + 1:]
    g = pl.program_id(1)
    kvw = FOX_KVH * FOX_HD
    half = g_pages * PAGE // 2

    @pl.when(g == 0)
    def _():
        _flash_init(m_ref, l_ref, acc_ref)
        carry_ref[...] = jnp.zeros(carry_ref.shape, F32)

    for i in range(g_pages):
        for h in range(FOX_KVH):
            sl = slice(h * FOX_HD, (h + 1) * FOX_HD)
            kbuf[i * PAGE:(i + 1) * PAGE, sl] = k_refs[i][pl.ds(h, PAGE, stride=FOX_KVH), :].astype(BF16)
            vbuf[i * PAGE:(i + 1) * PAGE, sl] = v_refs[i][pl.ds(h, PAGE, stride=FOX_KVH), :].astype(BF16)

    lf_t = jnp.concatenate([lf_refs[i][...] for i in range(g_pages)], axis=0)
    cum = _lane_cumsum(lf_t)
    tot = jnp.broadcast_to(cum[:, PAGE - 1:PAGE], cum.shape)
    inc = tot
    d = FOX_HEADS
    while d < g_pages * FOX_HEADS:
        inc = inc + jnp.concatenate([jnp.zeros((d, PAGE), F32), inc[:-d, :]], axis=0)
        d *= 2
    carry = carry_ref[...]
    c_keys = cum + (inc - tot) + jnp.concatenate([carry] * g_pages, axis=0)
    carry_ref[...] = carry + inc[(g_pages - 1) * FOX_HEADS:, :]

    q = q_ref[0]
    row_kvh = (lax.broadcasted_iota(jnp.int32, (128, kvw), 0) % FOX_HEADS) // FOX_GROUP
    col_kvh = lax.broadcasted_iota(jnp.int32, (128, kvw), 1) // FOX_HD
    own = row_kvh == col_kvh
    qbd = jnp.where(own, jnp.concatenate([q] * FOX_KVH, axis=1), jnp.zeros((), BF16))
    bias = jnp.concatenate(
        [jnp.concatenate([c_keys[i * FOX_HEADS:(i + 1) * FOX_HEADS, :]] * (128 // FOX_HEADS), axis=0)
         for i in range(g_pages)], axis=1)
    halves = {c: slice(c * half, (c + 1) * half) for c in range(2)}
    scores = {c: _dot_nt(qbd, kbuf[ks, :]) for c, ks in halves.items()}
    scores = {c: scores[c] - bias[:, ks] for c, ks in halves.items()}
    _flash_update(scores, {c: vbuf[ks, :] for c, ks in halves.items()}, m_ref, l_ref, acc_ref)

    @pl.when(g == n_pages // g_pages - 1)
    def _():
        c_new = carry_ref[...] + _lane_cumsum(lfnew_ref[0])
        bias_n = jnp.concatenate([c_new] * (128 // FOX_HEADS), axis=0)[:, 0:16]
        s = _dot_nt(qbd, knew_ref[0]) - bias_n
        tok = lax.broadcasted_iota(jnp.int32, s.shape, 0) // FOX_HEADS
        key = lax.broadcasted_iota(jnp.int32, s.shape, 1)
        s = jnp.where((key <= tok) & (key < n_new), s, NEG)
        _flash_update({0: s}, {0: vnew_ref[0]}, m_ref, l_ref, acc_ref)
        o = jnp.where(own, _flash_merge(m_ref, l_ref, acc_ref), 0.0)
        o = o[:, 0:128] + o[:, 128:256] + o[:, 256:384] + o[:, 384:512]
        o_ref[0] = o[0:n_new * FOX_HEADS, :].reshape(n_new, FOX_HEADS, FOX_HD)


def _fox_attn_sample(page_table, q_s, knew, vnew, lfnew_t, cache_k, cache_v, cache_lf_t, layer, n_new):
    n_b, n_pages = page_table.shape
    g_pages = PAGES_PER_STEP
    pt = page_table.reshape(-1)
    kvw = FOX_KVH * FOX_HD

    def page_spec(rows, width, i):
        return pl.BlockSpec((None, None, rows, width),
                            lambda b, g, pt_ref: (layer, pt_ref[b * n_pages + g * g_pages + i], 0, 0))

    per_b = lambda shape: pl.BlockSpec((1,) + shape, lambda b, g, pt_ref: (b, 0, 0))
    grid_spec = pltpu.PrefetchScalarGridSpec(
        num_scalar_prefetch=1,
        grid=(n_b, n_pages // g_pages),
        in_specs=[per_b((128, FOX_HD)), per_b((16, kvw)), per_b((16, kvw)), per_b((FOX_HEADS, 128))]
                 + [page_spec(PAGE * FOX_KVH, FOX_HD, i) for i in range(g_pages)]
                 + [page_spec(PAGE * FOX_KVH, FOX_HD, i) for i in range(g_pages)]
                 + [page_spec(FOX_HEADS, PAGE, i) for i in range(g_pages)],
        out_specs=pl.BlockSpec((1, n_new, FOX_HEADS, FOX_HD), lambda b, g, pt_ref: (b, 0, 0, 0)),
        scratch_shapes=[pltpu.VMEM((g_pages * PAGE, kvw), BF16), pltpu.VMEM((g_pages * PAGE, kvw), BF16),
                        pltpu.VMEM((FOX_HEADS, 128), F32)] + _flash_scratch(2, 128, kvw),
    )
    return pl.pallas_call(
        functools.partial(_fox_dec_kernel, n_pages=n_pages, n_new=n_new),
        grid_spec=grid_spec,
        out_shape=jax.ShapeDtypeStruct((n_b, n_new, FOX_HEADS, FOX_HD), F32),
        compiler_params=_params(("arbitrary", "arbitrary")),
        name="fox_attn_sample",
    )(pt, q_s, knew, vnew, lfnew_t, *([cache_k] * g_pages), *([cache_v] * g_pages), *([cache_lf_t] * g_pages))


def _rms_kernel(x_ref, g_ref, h_ref):
    h_ref[...] = _rms(x_ref[...], g_ref[...])


def _rms_norm(x, g):
    m = x.shape[0]
    row = pl.BlockSpec((TM, D_MODEL), lambda i: (i, 0))
    return pl.pallas_call(
        _rms_kernel, grid=(m // TM,), in_specs=[row, _resident((1, D_MODEL))], out_specs=row,
        out_shape=jax.ShapeDtypeStruct((m, D_MODEL), F32), compiler_params=_params(("arbitrary",)),
        name="rms_norm",
    )(x, g)


def _rwkv_mix_kernel(h_ref, hp_ref, mu_ref, w_ref, o_ref):
    h = h_ref[...]
    xm = h + (hp_ref[...] - h) * mu_ref[0]
    o_ref[0] = _dot(xm.astype(BF16), w_ref[0])


def _rwkv_mix_proj(h, hp, mu4, w4):
    m = h.shape[0]
    row = pl.BlockSpec((TM, D_MODEL), lambda p, i: (i, 0))
    return pl.pallas_call(
        _rwkv_mix_kernel,
        grid=(4, m // TM),
        in_specs=[row, row, pl.BlockSpec((1, 1, D_MODEL), lambda p, i: (p, 0, 0)),
                  pl.BlockSpec((1, D_MODEL, D_MODEL), lambda p, i: (p, 0, 0))],
        out_specs=pl.BlockSpec((1, TM, D_MODEL), lambda p, i: (p, i, 0)),
        out_shape=jax.ShapeDtypeStruct((4, m, D_MODEL), F32),
        compiler_params=_params(("arbitrary", "arbitrary")),
        name="rwkv_mix_proj",
    )(h, hp, mu4, w4)


def _split2(x):
    hi = x.astype(BF16)
    return hi, (x - hi.astype(F32)).astype(BF16)


def _head_sums(x):
    r = lax.broadcasted_iota(jnp.int32, (256, 128), 0)
    c = lax.broadcasted_iota(jnp.int32, (256, 128), 1)
    ones2 = ((r & 127) // RWKV_HD == c // RWKV_HD).astype(BF16)
    out = []
    for j in range(x.shape[1] // 128):
        hi, lo = _split2(x[:, j * 128:(j + 1) * 128])
        out.append(_dot(jnp.concatenate([hi, lo], axis=1), ones2))
    return jnp.concatenate(out, axis=1)


def _rwkv_post_kernel(h_ref, hp_ref, r_ref, k_ref, v_ref, muw_ref, mua_ref, w0_ref, w1_ref, w2_ref,
                      a0_ref, a1_ref, a2_ref, kk_ref, ka_ref, rk_ref, *outs, chunked):
    h = h_ref[...]
    dh = hp_ref[...] - h
    xw = (h + dh * muw_ref[...]).astype(BF16)
    xa = (h + dh * mua_ref[...]).astype(BF16)
    wl = w0_ref[...] + _dot(jnp.tanh(_dot(xw, w1_ref[...])).astype(BF16), w2_ref[...])
    lw = -jnp.exp(-_softplus(-wl) - 0.5)
    asig = _sigmoid(a0_ref[...] + _dot(_dot(xa, a1_ref[...]).astype(BF16), a2_ref[...]))
    k = k_ref[0]
    k4 = k * (1.0 + (asig - 1.0) * ka_ref[...])
    kr = k * kk_ref[...]
    kk = kr * lax.rsqrt(jnp.maximum(_head_sums(kr * kr), 1e-24))
    a_vec = -kk
    b_vec = kk * asig
    if not chunked:
        w_ref, k4_ref, a_ref, b_ref = outs
        w_ref[...] = jnp.exp(lw)
        k4_ref[...] = k4
        a_ref[...] = a_vec
        b_ref[...] = b_vec
        return
    x1_ref, x2_ref, bonus_ref, pin_ref = outs
    r = r_ref[0]
    ri = lax.broadcasted_iota(jnp.int32, (TM, TM), 0)
    ci = lax.broadcasted_iota(jnp.int32, (TM, TM), 1)
    tri = ((ri >= ci) & (ri // RWKV_CHUNK == ci // RWKV_CHUNK)).astype(BF16)
    hi = lw.astype(BF16)
    mid, lo = _split2(lw - hi.astype(F32))
    cl =_dot(jnp.concatenate([tri, tri, tri], axis=1), jnp.concatenate([hi, mid, lo], axis=0))
    p_in = jnp.exp(cl)
    p_inv = jnp.exp(-cl)
    at = (a_vec * jnp.exp(cl - lw)).astype(BF16)
    rt = (r * p_in).astype(BF16)
    bt = (b_vec * p_inv).astype(BF16)
    kt = (k4 * p_inv).astype(BF16)
    n = RWKV_CHUNK
    for c in range(TM // n):
        x1_ref[2 * n * c:2 * n * c + n, :] = at[n * c:n * (c + 1), :]
        x1_ref[2 * n * c + n:2 * n * (c + 1), :] = rt[n * c:n * (c + 1), :]
        x2_ref[2 * n * c:2 * n * c + n, :] = bt[n * c:n * (c + 1), :]
        x2_ref[2 * n * c + n:2 * n * (c + 1), :] = kt[n * c:n * (c + 1), :]
    bonus_ref[...] = _head_sums(r * k4 * rk_ref[...]) * v_ref[0]
    pin_ref[...] = p_in


def _rwkv_post(h, hp, rkvg, vecs, w1, w2, a1, a2, row0, n_rows, chunked):
    t0 = row0 // TM
    row = pl.BlockSpec((TM, D_MODEL), lambda i: (i + t0, 0))
    proj = lambda p: pl.BlockSpec((1, TM, D_MODEL), lambda i: (p, i + t0, 0))
    out_row = pl.BlockSpec((TM, D_MODEL), lambda i: (i, 0))
    vec = _resident((1, D_MODEL))
    muw, mua, w0, a0, kk, ka, rk = vecs
    if chunked:
        stacked = pl.BlockSpec((2 * TM, D_MODEL), lambda i: (i, 0))
        out_specs = [stacked, stacked, out_row, out_row]
        out_shape = [jax.ShapeDtypeStruct((2 * n_rows, D_MODEL), BF16)] * 2 + \
                    [jax.ShapeDtypeStruct((n_rows, D_MODEL), F32)] * 2
    else:
        out_specs = [out_row] * 4
        out_shape = [jax.ShapeDtypeStruct((n_rows, D_MODEL), F32)] * 4
    return pl.pallas_call(
        functools.partial(_rwkv_post_kernel, chunked=chunked),
        grid=(n_rows // TM,),
        in_specs=[row, row, proj(0), proj(1), proj(2), vec, vec, vec, _resident(w1.shape), _resident(w2.shape),
                  vec, _resident(a1.shape), _resident(a2.shape), vec, vec, vec],
        out_specs=out_specs,
        out_shape=out_shape,
        compiler_params=_params(("arbitrary",)),
        name="rwkv_post_chunked" if chunked else "rwkv_post_steps",
    )(h, hp, rkvg, rkvg, rkvg, muw, mua, w0, w1, w2, a0, a1, a2, kk, ka, rk)


def _rwkv_chunks(states, x1s, x2s, vs):
    heads = range(len(vs))
    n = vs[0].shape[0]
    row = lax.broadcasted_iota(jnp.int32, (2 * n, 2 * n), 0)
    col = lax.broadcasted_iota(jnp.int32, (2 * n, 2 * n), 1) & (n - 1)
    keep = (row > col) & ((row < n) | (row - n >= col))
    zeros = jnp.zeros((n, RWKV_HD), F32)
    gs = [jnp.where(keep, _dot_nt(x1s[h], x2s[h]), 0.0).astype(BF16) for h in heads]
    zs = [_dot_nt(x1s[h], states[h].astype(BF16)) for h in heads]
    us = [zs[h][0:n] + _dot(gs[h][0:n], jnp.concatenate([zeros, vs[h]], axis=0).astype(BF16)) for h in heads]
    nks = [gs[h][0:n, 0:n] for h in heads]
    levels = int(math.log2(n))
    for lvl in range(levels):
        us = [us[h] + _dot(nks[h], us[h].astype(BF16)) for h in heads]
        if lvl + 1 < levels:
            nks = [_dot(nks[h], nks[h]).astype(BF16) for h in heads]
    uvs = [jnp.concatenate([us[h], vs[h]], axis=0).astype(BF16) for h in heads]
    ys = [zs[h][n:2 * n] + _dot(gs[h][n:2 * n], uvs[h]) for h in heads]
    new_states = [states[h] + lax.dot_general(uvs[h], x2s[h], TN, preferred_element_type=F32) for h in heads]
    return new_states, ys


def _rwkv_scan_kernel(x1_ref, x2_ref, v_ref, bonus_ref, pin_ref, lnw_ref, lnb_ref,
                      u_ref, sout_ref, state_ref, *, heads, n_chunks):
    c = pl.program_id(2)

    @pl.when(c == 0)
    def _():
        state_ref[...] = jnp.zeros(state_ref.shape, F32)

    sls = [slice(h * RWKV_HD, (h + 1) * RWKV_HD) for h in range(heads)]
    new_states, ys = _rwkv_chunks([state_ref[h] for h in range(heads)], [x1_ref[:, sl] for sl in sls],
                                  [x2_ref[:, sl] for sl in sls], [v_ref[0, :, sl] for sl in sls])
    outs = []
    for h, sl in enumerate(sls):
        state_ref[h] = new_states[h] * pin_ref[7:8, sl]
        y = ys[h]
        mean = jnp.mean(y, axis=-1, keepdims=True)
        var = jnp.mean(jnp.square(y - mean), axis=-1, keepdims=True)
        outs.append((y - mean) * lax.rsqrt(var + RWKV_LNX_EPS) * lnw_ref[:, sl] + lnb_ref[:, sl] + bonus_ref[:, sl])
    u_ref[...] = jnp.concatenate(outs, axis=-1)

    @pl.when(c == n_chunks - 1)
    def _():
        sout_ref[0] = state_ref[...]


def _rwkv_scan(x1, x2, rkvg, bonus, pin, lnw, lnb, n_seq, seq):
    n = RWKV_CHUNK
    heads = RWKV_SCAN_HEADS
    n_chunks = seq // n
    w = heads * RWKV_HD
    tok = lambda rows, scale: pl.BlockSpec((rows, w), lambda s, hg, c: (scale * (s * n_chunks + c), hg))
    last8 = pl.BlockSpec((8, w), lambda s, hg, c: ((s * n_chunks + c + 1) * (n // 8) - 1, hg))
    vec = pl.BlockSpec((1, w), lambda s, hg, c: (0, hg))
    return pl.pallas_call(
        functools.partial(_rwkv_scan_kernel, heads=heads, n_chunks=n_chunks),
        grid=(n_seq, RWKV_HEADS // heads, n_chunks),
        in_specs=[tok(2 * n, 1), tok(2 * n, 1),
                  pl.BlockSpec((1, n, w), lambda s, hg, c: (2, s * n_chunks + c, hg)),
                  tok(n, 1), last8, vec, vec],
        out_specs=[tok(n, 1), pl.BlockSpec((1, heads, RWKV_HD, RWKV_HD), lambda s, hg, c: (s, hg, 0, 0))],
        out_shape=[jax.ShapeDtypeStruct((n_seq * seq, D_MODEL), F32),
                   jax.ShapeDtypeStruct((n_seq, RWKV_HEADS, RWKV_HD, RWKV_HD), F32)],
        scratch_shapes=[pltpu.VMEM((heads, RWKV_HD, RWKV_HD), F32)],
        compiler_params=_params(("arbitrary", "arbitrary", "arbitrary")),
        name="rwkv_scan",
    )(x1, x2, rkvg, bonus, pin, lnw, lnb)


def _rwkv_steps_kernel(r_ref, w_ref, k_ref, v_ref, a_ref, b_ref, s0_ref, rk_ref, lnw_ref, lnb_ref,
                       u_ref, sout_ref, y_ref, *, heads, n_new):
    for h in range(heads):
        hs = slice(h * RWKV_HD, (h + 1) * RWKV_HD)

        def body(vi, carry):
            s = s0_ref[h, vi]
            for t in range(n_new):
                sa = jnp.sum(s * a_ref[t, hs, :], axis=0, keepdims=True)
                vv = v_ref[t, pl.ds(h * RWKV_HD + vi, 1), :]
                s = s * w_ref[t, hs, :] + sa * b_ref[t, hs, :] + vv * k_ref[t, hs, :]
                y_ref[t, pl.ds(h * RWKV_HD + vi, 1), :] = jnp.sum(s * r_ref[t, hs, :], axis=0, keepdims=True)
            sout_ref[h, vi] = s
            return carry

        lax.fori_loop(0, RWKV_HD, body, 0)
        for t in range(n_new):
            y = y_ref[t, hs, :]
            mean = jnp.mean(y, axis=0, keepdims=True)
            var = jnp.mean(jnp.square(y - mean), axis=0, keepdims=True)
            yn = (y - mean) * lax.rsqrt(var + RWKV_LNX_EPS) * lnw_ref[hs, :] + lnb_ref[hs, :]
            bonus = jnp.sum(r_ref[t, hs, :] * k_ref[t, hs, :] * rk_ref[hs, :], axis=0, keepdims=True)
            u_ref[t, hs, :] = yn + bonus * v_ref[t, hs, :]


def _rwkv_steps(r, w, k, v, a, b, s0_t, rk, lnw, lnb):
    n_new, _, n_db = r.shape
    heads = 2
    rows = heads * RWKV_HD
    tok = pl.BlockSpec((n_new, rows, n_db), lambda g: (0, g, 0))
    vec = pl.BlockSpec((rows, n_db), lambda g: (g, 0))
    st = pl.BlockSpec((heads, RWKV_HD, RWKV_HD, n_db), lambda g: (g, 0, 0, 0))
    return pl.pallas_call(
        functools.partial(_rwkv_steps_kernel, heads=heads, n_new=n_new),
        grid=(RWKV_HEADS // heads,),
        in_specs=[tok] * 6 + [st, vec, vec, vec],
        out_specs=[tok, st],
        out_shape=[jax.ShapeDtypeStruct((n_new, D_MODEL, n_db), F32),
                   jax.ShapeDtypeStruct((RWKV_HEADS, RWKV_HD, RWKV_HD, n_db), F32)],
        scratch_shapes=[pltpu.VMEM((n_new, rows, n_db), F32)],
        compiler_params=_params(("arbitrary",)),
        name="rwkv_steps",
    )(r, w, k, v, a, b, s0_t, rk, lnw, lnb)


def _out_proj_kernel(u_ref, g_ref, x_ref, w_ref, gp_ref, o_ref):
    g = g_ref[...]
    a = (u_ref[...] * (g * _sigmoid(g))).astype(BF16)
    o_ref[...] = x_ref[...] + _rms(_dot(a, w_ref[...]), gp_ref[...])


def _out_proj(u, gate, x, w_o, gpost):
    m = x.shape[0]
    row = pl.BlockSpec((TM, D_MODEL), lambda i: (i, 0))
    return pl.pallas_call(
        _out_proj_kernel, grid=(m // TM,),
        in_specs=[row, row, row, _resident(w_o.shape), _resident((1, D_MODEL))], out_specs=row,
        out_shape=jax.ShapeDtypeStruct((m, D_MODEL), F32), compiler_params=_params(("arbitrary",)),
        name="out_proj",
    )(u, gate, x, w_o, gpost)


def _mla_layer(x, dims, page_table, cache_ckv, cache_kpe, layer, tables, gpre, w_in, q_norm, kv_norm, w_uq, w_ukv):
    n_b, seq, n_db, n_new = dims
    mp = n_b * seq
    w_main = jnp.concatenate(
        [w_in[:, 0:1024], w_in[:, 1024:1088],
         -w_in[:, 1056:1088], w_in[:, 1024:1056], w_in[:, 1088:]], axis=1).astype(BF16)
    uq3 = w_uq.reshape(MLA_LORA, MLA_HEADS, MLA_NOPE + MLA_ROPE)
    wuqn = uq3[:, :, 0:MLA_NOPE].reshape(MLA_LORA, -1).astype(BF16)
    pe = uq3[:, :, MLA_NOPE:]
    wuqp = pe.reshape(MLA_LORA, -1).astype(BF16)
    wuqr = jnp.concatenate([-pe[:, :, 32:], pe[:, :, :32]], axis=-1).reshape(MLA_LORA, -1).astype(BF16)
    wk = jnp.transpose(w_ukv[:, :, 0:MLA_NOPE], (1, 2, 0)).astype(BF16)
    wv = jnp.transpose(w_ukv[:, :, MLA_NOPE:], (1, 0, 2)).astype(BF16)
    gate, ckv, kpe, kcat, qfull = _mla_proj(x, gpre, w_main, q_norm, kv_norm, wuqn, wuqp, wuqr, wk, *tables)
    u_p = _mla_attn_prompt(qfull, kcat, wv, n_b, seq)
    q_s = jnp.transpose(qfull[:, mp:, :], (1, 0, 2)).reshape(n_db, n_new * MLA_HEADS, MLA_KW)
    q_s = jnp.pad(q_s, ((0, 0), (0, 128 - n_new * MLA_HEADS), (0, 0)))
    knew = jnp.pad(kcat[mp:].reshape(n_db, n_new, MLA_KW), ((0, 0), (0, 16 - n_new), (0, 0)))
    o_lat = _mla_attn_sample(page_table, q_s, knew, cache_ckv, jnp.swapaxes(cache_kpe, 2, 3), layer, n_new)
    u_s = _mla_up(o_lat.reshape(n_db * n_new, MLA_HEADS * MLA_LORA), wv)
    return jnp.concatenate([u_p, u_s], axis=0), gate, ckv, kpe


def _fox_layer(x, dims, page_table, cache_k, cache_v, cache_lf, layer, gpre, w_in, b_f, q_norm, k_norm):
    n_b, seq, n_db, n_new = dims
    mp = n_b * seq
    kvw = FOX_KVH * FOX_HD
    nq = FOX_HEADS * FOX_HD
    w_main = jnp.concatenate(
        [w_in[:, 0:nq + 2 * kvw], w_in[:, nq + 2 * kvw:nq + 2 * kvw + FOX_HEADS],
         jnp.zeros((D_MODEL, 128 - FOX_HEADS), F32), w_in[:, nq + 2 * kvw + FOX_HEADS:]], axis=1).astype(BF16)
    bf = jnp.pad(b_f, (0, 128 - FOX_HEADS)).reshape(1, 128)
    q, k, v, kb, vb, lf, c, gate = _fox_proj(x, gpre, w_main, bf, q_norm.reshape(1, -1), k_norm.reshape(1, -1),
                                             seq // TM)
    c_t = jnp.transpose(c[:mp].reshape(n_b, seq, FOX_HEADS), (0, 2, 1)).reshape(n_b, FOX_KVH, FOX_GROUP, seq)
    u_p = _fox_attn_prompt(q, kb, vb, c_t, n_b, seq)
    q_s = jnp.pad(q[mp:].reshape(n_db, n_new * FOX_HEADS, FOX_HD), ((0, 0), (0, 128 - n_new * FOX_HEADS), (0, 0)))
    knew = jnp.pad(kb[mp:].reshape(n_db, n_new, kvw), ((0, 0), (0, 16 - n_new), (0, 0)))
    vnew = jnp.pad(vb[mp:].reshape(n_db, n_new, kvw), ((0, 0), (0, 16 - n_new), (0, 0)))
    lfnew_t = jnp.pad(jnp.transpose(lf[mp:].reshape(n_db, n_new, FOX_HEADS), (0, 2, 1)),
                      ((0, 0), (0, 0), (0, 128 - n_new)))
    n_pool = cache_k.shape[1]
    o_s = _fox_attn_sample(page_table, q_s, knew, vnew, lfnew_t,
                           cache_k.reshape(cache_k.shape[0], n_pool, PAGE * FOX_KVH, FOX_HD),
                           cache_v.reshape(cache_v.shape[0], n_pool, PAGE * FOX_KVH, FOX_HD),
                           jnp.swapaxes(cache_lf, 2, 3), layer, n_new)
    u = jnp.concatenate([u_p, o_s.reshape(n_db * n_new, D_MODEL)], axis=0)
    return u, gate, k, v, lf


def _rwkv_layer(x, dims, shift0, s0, gpre, mu, w_rkvg, w0, w1, w2, a0, a1, a2, k_k, k_a, r_k, lnx_w, lnx_b):
    n_b, seq, n_db, n_new = dims
    mp = n_b * seq
    vec = lambda a: a.reshape(1, D_MODEL)
    h = _rms_norm(x, gpre)
    h_p = h[:mp].reshape(n_b, seq, D_MODEL)
    h_s = h[mp:].reshape(n_db, n_new, D_MODEL)
    hp = jnp.concatenate(
        [jnp.concatenate([jnp.zeros((n_b, 1, D_MODEL), F32), h_p[:, :-1]], axis=1).reshape(mp, D_MODEL),
         jnp.concatenate([shift0[:, None, :], h_s[:, :-1]], axis=1).reshape(n_db * n_new, D_MODEL)], axis=0)
    rkvg = _rwkv_mix_proj(h, hp, mu[0:4].reshape(4, 1, D_MODEL), w_rkvg.astype(BF16))
    lora = w1.shape[1]
    w1p = jnp.pad(w1, ((0, 0), (0, RWKV_LORA_PAD - lora))).astype(BF16)
    w2p = jnp.pad(w2, ((0, RWKV_LORA_PAD - lora), (0, 0))).astype(BF16)
    a1p = jnp.pad(a1, ((0, 0), (0, RWKV_LORA_PAD - lora))).astype(BF16)
    a2p = jnp.pad(a2, ((0, RWKV_LORA_PAD - lora), (0, 0))).astype(BF16)
    vecs = (vec(mu[4]), vec(mu[5]), vec(w0), vec(a0), vec(k_k), vec(k_a), vec(r_k))
    x1, x2, bonus, pin = _rwkv_post(h, hp, rkvg, vecs, w1p, w2p, a1p, a2p, 0, mp, True)
    u_p, s_p = _rwkv_scan(x1, x2, rkvg, bonus, pin, vec(lnx_w), vec(lnx_b), n_b, seq)
    w_s, k_s, a_s, b_s = _rwkv_post(h, hp, rkvg, vecs, w1p, w2p, a1p, a2p, mp, n_db * n_new, False)
    to_lanes = lambda a: jnp.transpose(a.reshape(n_db, n_new, D_MODEL), (1, 2, 0))
    col = lambda a: jnp.broadcast_to(a.reshape(D_MODEL, 1), (D_MODEL, n_db))
    u_t, s_t = _rwkv_steps(to_lanes(rkvg[0, mp:]), to_lanes(w_s), to_lanes(k_s), to_lanes(rkvg[2, mp:]),
                           to_lanes(a_s), to_lanes(b_s), jnp.transpose(s0, (1, 2, 3, 0)),
                           col(r_k), col(lnx_w), col(lnx_b))
    u_s = jnp.transpose(u_t, (2, 0, 1)).reshape(n_db * n_new, D_MODEL)
    u = jnp.concatenate([u_p, u_s], axis=0)
    return u, rkvg[3], s_p, jnp.transpose(s_t, (3, 0, 1, 2)), h_p[:, -1], h_s[:, -1]


def _rope_tables(n_b, seq, n_db, n_new, n_past):
    half = MLA_ROPE // 2
    inv = ROPE_THETA ** (-jnp.arange(half, dtype=F32) / half)
    pos = jnp.concatenate([jnp.tile(jnp.arange(seq, dtype=F32), n_b),
                           jnp.tile(n_past + jnp.arange(n_new, dtype=F32), n_db)])
    ang = pos[:, None] * inv[None, :]
    return jnp.tile(jnp.cos(ang), (1, 4)), jnp.tile(jnp.sin(ang), (1, 4))


def kernel(x_prompt, x_sample, cache_mla_ckv, cache_mla_kpe, cache_fox_k, cache_fox_v, cache_fox_logf,
           state_rwkv_wkv, state_rwkv_shift, page_table, norm_pre, norm_post,
           mla_w_in, mla_q_norm, mla_kv_norm, mla_w_uq, mla_w_ukv, mla_w_o,
           fox_w_in, fox_b_f, fox_q_norm, fox_k_norm, fox_w_o,
           rwkv_mu, rwkv_w_rkvg, rwkv_w0, rwkv_w1, rwkv_w2, rwkv_a0, rwkv_a1, rwkv_a2,
           rwkv_k_k, rwkv_k_a, rwkv_r_k, rwkv_lnx_w, rwkv_lnx_b, rwkv_w_o):
    n_b, seq, _ = x_prompt.shape
    n_db, n_new, _ = x_sample.shape
    n_pages = page_table.shape[1]
    mp, ms = n_b * seq, n_db * n_new
    assert seq % TQ == 0 and seq % TM == 0 and ms % TM == 0 and n_pages % PAGES_PER_STEP == 0
    assert n_new <= 8 and n_new * FOX_HEADS <= 128
    dims = (n_b, seq, n_db, n_new)
    depth = norm_pre.shape[0]
    x = jnp.concatenate([x_prompt.reshape(mp, D_MODEL), x_sample.reshape(ms, D_MODEL)], axis=0)
    tables = _rope_tables(n_b, seq, n_db, n_new, n_pages * PAGE)
    row = lambda a: a.reshape(1, -1)
    outs = {name: [] for name in ("ckv", "kpe", "k", "v", "lf", "wkv_p", "wkv_s", "sh_p", "sh_s")}
    for i in range(depth):
        j, kind = divmod(i, 3)
        gpre = row(norm_pre[i])
        if kind == 0:
            u, gate, ckv, kpe = _mla_layer(x, dims, page_table, cache_mla_ckv, cache_mla_kpe, j, tables, gpre,
                                           mla_w_in[j], row(mla_q_norm[j]), row(mla_kv_norm[j]),
                                           mla_w_uq[j], mla_w_ukv[j])
            w_o = mla_w_o[j]
            outs["ckv"].append(ckv)
            outs["kpe"].append(kpe)
        elif kind == 1:
            u, gate, k, v, lf = _fox_layer(x, dims, page_table, cache_fox_k, cache_fox_v, cache_fox_logf, j, gpre,
                                           fox_w_in[j], fox_b_f[j], fox_q_norm[j], fox_k_norm[j])
            w_o = fox_w_o[j]
            outs["k"].append(k)
            outs["v"].append(v)
            outs["lf"].append(lf)
        else:
            u, gate, s_p, s_s, sh_p, sh_s = _rwkv_layer(
                x, dims, state_rwkv_shift[j], state_rwkv_wkv[j], gpre, rwkv_mu[j], rwkv_w_rkvg[j],
                rwkv_w0[j], rwkv_w1[j], rwkv_w2[j], rwkv_a0[j], rwkv_a1[j], rwkv_a2[j],
                rwkv_k_k[j], rwkv_k_a[j], rwkv_r_k[j].reshape(-1), rwkv_lnx_w[j], rwkv_lnx_b[j])
            w_o = rwkv_w_o[j]
            outs["wkv_p"].append(s_p)
            outs["wkv_s"].append(s_s)
            outs["sh_p"].append(sh_p)
            outs["sh_s"].append(sh_s)
        x = _out_proj(u, gate, x, w_o.astype(BF16), row(norm_post[i]))

    def split(name, *tail):
        a = jnp.stack(outs[name])
        return (a[:, :mp].reshape((a.shape[0], n_b, seq) + tail),
                a[:, mp:].reshape((a.shape[0], n_db, n_new) + tail))

    p_ckv, s_ckv = split("ckv", MLA_LORA)
    p_kpe, s_kpe = split("kpe", MLA_ROPE)
    p_k, s_k = split("k", FOX_KVH, FOX_HD)
    p_v, s_v = split("v", FOX_KVH, FOX_HD)
    p_lf, s_lf = split("lf", FOX_HEADS)
    return (x[:mp].reshape(n_b, seq, D_MODEL), x[mp:].reshape(n_db, n_new, D_MODEL),
            p_ckv, p_kpe, p_k, p_v, p_lf, jnp.stack(outs["wkv_p"]), jnp.stack(outs["sh_p"]),
            s_ckv, s_kpe, s_k, s_v, s_lf, jnp.stack(outs["wkv_s"]), jnp.stack(outs["sh_s"]))
```

```python
import functools
import math

import jax
import jax.numpy as jnp
from jax import lax
from jax.experimental import pallas as pl
from jax.experimental.pallas import tpu as pltpu

F32 = jnp.float32
BF16 = jnp.bfloat16

D_MODEL = 2048
NORM_EPS = 1e-6
PAGE = 128

MLA_NOPE = 128
MLA_ROPE = 64
MLA_V = 128
MLA_HEADS = 16
MLA_HEAD_GROUP = 4
MLA_LORA = 512
MLA_SCALE = (MLA_NOPE + MLA_ROPE) ** -0.5
MLA_KW = MLA_LORA + 128
ROPE_THETA = 10000.0

FOX_HD = 128
FOX_HEADS = 16
FOX_KVH = 4
FOX_GROUP = 4
FOX_SCALE = FOX_HD ** -0.5

RWKV_HD = 64
RWKV_HEADS = 32
RWKV_LORA_PAD = 128
RWKV_CHUNK = 64
RWKV_SCAN_HEADS = 8
RWKV_LNX_EPS = 64e-5

TM = 256
TQ = 256
PAGES_PER_STEP = 16
DEC_CHAINS = 4
NEG = -1e30
VMEM_LIMIT = 56 * 1024 * 1024
HI = lax.Precision.HIGHEST

NT = (((1,), (1,)), ((), ()))
TN = (((0,), (0,)), ((), ()))


def _params(sem):
    return pltpu.CompilerParams(dimension_semantics=sem, vmem_limit_bytes=VMEM_LIMIT)


def _resident(shape):
    nd = len(shape)
    return pl.BlockSpec(shape, lambda *_: (0,) * nd, pipeline_mode=pl.Buffered(1))


def _dot(a, b):
    return jnp.dot(a, b, preferred_element_type=F32)


def _dot_nt(a, b):
    return lax.dot_general(a, b, NT, preferred_element_type=F32)


def _rms(x, g):
    return x * lax.rsqrt(jnp.mean(x * x, axis=-1, keepdims=True) + NORM_EPS) * g


def _sigmoid(x):
    return 1.0 / (1.0 + jnp.exp(-x))


def _softplus(x):
    return jnp.maximum(x, 0.0) + jnp.log(1.0 + jnp.exp(-jnp.abs(x)))


def _mla_proj_kernel(x_ref, gpre_ref, w_ref, qn_ref, kvn_ref, wuqn_ref, wuqp_ref, wuqr_ref, wk_ref,
                     cos_ref, sin_ref, gate_ref, ckv_ref, kpe_ref, kcat_ref, qfull_ref):
    hb = _rms(x_ref[...], gpre_ref[...]).astype(BF16)
    cq = _rms(_dot(hb, w_ref[:, 0:512]), qn_ref[...]).astype(BF16)
    ckv = _rms(_dot(hb, w_ref[:, 512:1024]), kvn_ref[...])
    z = _dot(hb, w_ref[:, 1024:1152])
    cos_t = cos_ref[...]
    sin_t = sin_ref[...]
    lane = lax.broadcasted_iota(jnp.int32, z.shape, 1)
    kpe = jnp.where(lane < MLA_ROPE, z * cos_t + pltpu.roll(z, 64, 1) * sin_t, 0.0)
    gate_ref[...] = _dot(hb, w_ref[:, 1152:3200])
    ckv_ref[...] = ckv
    kpe_ref[...] = kpe[:, 0:MLA_ROPE]
    kcat_ref[:, 0:512] = ckv.astype(BF16)
    kcat_ref[:, 512:640] = kpe.astype(BF16)
    qn = _dot(cq, wuqn_ref[...]).astype(BF16)
    cos8 = jnp.concatenate([cos_t] * 8, axis=1)
    sin8 = jnp.concatenate([sin_t] * 8, axis=1)
    qpe = (_dot(cq, wuqp_ref[...]) * cos8 + _dot(cq, wuqr_ref[...]) * sin8) * MLA_SCALE
    for h in range(MLA_HEADS):
        ql = _dot(qn[:, h * 128:(h + 1) * 128], wk_ref[h]) * MLA_SCALE
        qfull_ref[h, :, 0:512] = ql.astype(BF16)
        blk = qpe[:, (h // 2) * 128:(h // 2 + 1) * 128]
        if h % 2:
            blk = pltpu.roll(blk, 64, 1)
        qfull_ref[h, :, 512:640] = jnp.where(lane < MLA_ROPE, blk, 0.0).astype(BF16)


def _mla_proj(x, gpre, w_main, qn, kvn, wuqn, wuqp, wuqr, wk, cos_t, sin_t):
    m = x.shape[0]
    row = lambda w: pl.BlockSpec((TM, w), lambda i: (i, 0))
    return pl.pallas_call(
        _mla_proj_kernel,
        grid=(m // TM,),
        in_specs=[row(D_MODEL), _resident((1, D_MODEL)), _resident(w_main.shape), _resident((1, 512)),
                  _resident((1, 512)), _resident(wuqn.shape), _resident(wuqp.shape), _resident(wuqr.shape),
                  _resident(wk.shape), row(128), row(128)],
        out_specs=[row(D_MODEL), row(512), row(MLA_ROPE), row(MLA_KW),
                   pl.BlockSpec((MLA_HEADS, TM, MLA_KW), lambda i: (0, i, 0))],
        out_shape=[jax.ShapeDtypeStruct((m, D_MODEL), F32), jax.ShapeDtypeStruct((m, 512), F32),
                   jax.ShapeDtypeStruct((m, MLA_ROPE), F32), jax.ShapeDtypeStruct((m, MLA_KW), BF16),
                   jax.ShapeDtypeStruct((MLA_HEADS, m, MLA_KW), BF16)],
        compiler_params=_params(("arbitrary",)),
        name="mla_proj",
    )(x, gpre, w_main, qn, kvn, wuqn, wuqp, wuqr, wk, cos_t, sin_t)


def _flash_update(scores, values, m_ref, l_ref, acc_ref, mxu_row_sum=False):
    probs, alphas = {}, {}
    for c, s in scores.items():
        m_prev = m_ref[c]
        m_new = jnp.maximum(m_prev, jnp.max(s, axis=-1, keepdims=True))
        alphas[c] = jnp.exp(m_prev - m_new)
        p = jnp.exp(s - m_new)
        if not mxu_row_sum:
            l_ref[c] = alphas[c] * l_ref[c] + jnp.sum(p, axis=-1, keepdims=True)
        m_ref[c] = m_new
        probs[c] = p.astype(BF16)
    for c in scores:
        acc_ref[c] = alphas[c] * acc_ref[c] + _dot(probs[c], values[c])
        if mxu_row_sum:
            ones = jnp.ones((probs[c].shape[1], 128), BF16)
            l_ref[c] = alphas[c] * l_ref[c] + _dot(probs[c], ones)[:, 0:1]


def _flash_init(m_ref, l_ref, acc_ref):
    m_ref[...] = jnp.full(m_ref.shape, NEG, F32)
    l_ref[...] = jnp.zeros(l_ref.shape, F32)
    acc_ref[...] = jnp.zeros(acc_ref.shape, F32)


def _flash_merge(m_ref, l_ref, acc_ref):
    n = m_ref.shape[0]
    m = functools.reduce(jnp.maximum, [m_ref[c] for c in range(n)])
    w = [jnp.exp(m_ref[c] - m) for c in range(n)]
    return sum(acc_ref[c] * w[c] for c in range(n)) / sum(l_ref[c] * w[c] for c in range(n))


def _flash_scratch(chains, rows, width):
    return [pltpu.VMEM((chains, rows, width), F32), pltpu.VMEM((chains, rows, 1), F32),
            pltpu.VMEM((chains, rows, 1), F32)]


def _mla_attn_prompt_kernel(q_ref, k_ref, wv_ref, o_ref, acc_ref, m_ref, l_ref):
    qi = pl.program_id(1)
    half = MLA_HEAD_GROUP // 2
    rows = half * TQ
    qs = [q_ref[c * half:(c + 1) * half].reshape(rows, MLA_KW) for c in range(2)]
    _flash_init(m_ref, l_ref, acc_ref)

    def keys(kb):
        return k_ref[pl.ds(pl.multiple_of(kb * TQ, TQ), TQ), :]

    def qk(kb):
        k = keys(kb)
        return tuple(_dot_nt(qs[c], k) for c in range(2))

    def update(kb, scores):
        v = keys(kb)[:, 0:MLA_LORA]
        _flash_update(dict(enumerate(scores)), {0: v, 1: v}, m_ref, l_ref, acc_ref)

    def body(kb, scores):
        nxt = qk(kb + 1)
        update(kb, scores)
        return nxt

    scores = lax.fori_loop(0, qi, body, qk(0))
    tok = lax.broadcasted_iota(jnp.int32, (rows, TQ), 0) & (TQ - 1)
    key = lax.broadcasted_iota(jnp.int32, (rows, TQ), 1)
    update(qi, tuple(jnp.where(key <= tok, s, NEG) for s in scores))
    for c in range(2):
        o = (acc_ref[c] / l_ref[c]).astype(BF16)
        for hh in range(half):
            h = c * half + hh
            o_ref[:, h * MLA_V:(h + 1) * MLA_V] = _dot(o[hh * TQ:(hh + 1) * TQ, :], wv_ref[h])


def _mla_attn_prompt(qfull, kcat, wv, n_batch, seq):
    nq = seq // TQ
    hg = MLA_HEAD_GROUP
    rows = hg // 2 * TQ
    return pl.pallas_call(
        _mla_attn_prompt_kernel,
        grid=(n_batch, nq, MLA_HEADS // hg),
        in_specs=[pl.BlockSpec((hg, TQ, MLA_KW), lambda b, q, g: (g, b * nq + q, 0)),
                  pl.BlockSpec((seq, MLA_KW), lambda b, q, g: (b, 0)),
                  pl.BlockSpec((hg, MLA_LORA, MLA_V), lambda b, q, g: (g, 0, 0))],
        out_specs=pl.BlockSpec((TQ, hg * MLA_V), lambda b, q, g: (b * nq + q, g)),
        out_shape=jax.ShapeDtypeStruct((n_batch * seq, D_MODEL), F32),
        scratch_shapes=_flash_scratch(2, rows, MLA_LORA),
        compiler_params=_params(("arbitrary", "arbitrary", "arbitrary")),
        name="mla_attn_prompt",
    )(qfull, kcat, wv)


def _mla_dec_kernel(pt_ref, q_ref, knew_ref, *rest, n_pages, n_new):
    g_pages = PAGES_PER_STEP
    ckv_refs = rest[:g_pages]
    kpe_refs = rest[g_pages:2 * g_pages]
    o_ref = rest[2 * g_pages]
    kbuf, kpbuf, acc_ref, m_ref, l_ref = rest[2 * g_pages + 1:]
    g = pl.program_id(1)
    part = g_pages * PAGE // DEC_CHAINS

    @pl.when(g == 0)
    def _():
        _flash_init(m_ref, l_ref, acc_ref)

    for i in range(g_pages):
        kbuf[i * PAGE:(i + 1) * PAGE, :] = ckv_refs[i][...].astype(BF16)
        kpbuf[:, i * PAGE:(i + 1) * PAGE] = kpe_refs[i][...].astype(BF16)
    q = q_ref[0]
    q_lat = q[:, 0:MLA_LORA]
    q_pe = q[:, MLA_LORA:MLA_LORA + MLA_ROPE]
    keys = {c: kbuf[c * part:(c + 1) * part, :] for c in range(DEC_CHAINS)}
    scores = {c: _dot_nt(q_lat, k) + _dot(q_pe, kpbuf[:, c * part:(c + 1) * part]) for c, k in keys.items()}
    _flash_update(scores, keys, m_ref, l_ref, acc_ref)

    @pl.when(g == n_pages // g_pages - 1)
    def _():
        kn = knew_ref[0]
        s = _dot_nt(q, kn)
        tok = lax.broadcasted_iota(jnp.int32, s.shape, 0) // MLA_HEADS
        key = lax.broadcasted_iota(jnp.int32, s.shape, 1)
        s = jnp.where((key <= tok) & (key < n_new), s, NEG)
        _flash_update({0: s}, {0: kn[:, 0:MLA_LORA]}, m_ref, l_ref, acc_ref)
        o = _flash_merge(m_ref, l_ref, acc_ref)
        o_ref[0] = o[0:n_new * MLA_HEADS, :].reshape(n_new, MLA_HEADS, MLA_LORA)


def _mla_attn_sample(page_table, q_s, knew, cache_ckv, cache_kpe_t, layer, n_new):
    n_b, n_pages = page_table.shape
    g_pages = PAGES_PER_STEP
    pt = page_table.reshape(-1)

    def page_spec(rows, width, i):
        return pl.BlockSpec((None, None, rows, width),
                            lambda b, g, pt_ref: (layer, pt_ref[b * n_pages + g * g_pages + i], 0, 0))

    grid_spec = pltpu.PrefetchScalarGridSpec(
        num_scalar_prefetch=1,
        grid=(n_b, n_pages // g_pages),
        in_specs=[pl.BlockSpec((1, 128, MLA_KW), lambda b, g, pt_ref: (b, 0, 0)),
                  pl.BlockSpec((1, 16, MLA_KW), lambda b, g, pt_ref: (b, 0, 0))]
                 + [page_spec(PAGE, MLA_LORA, i) for i in range(g_pages)]
                 + [page_spec(MLA_ROPE, PAGE, i) for i in range(g_pages)],
        out_specs=pl.BlockSpec((1, n_new, MLA_HEADS, MLA_LORA), lambda b, g, pt_ref: (b, 0, 0, 0)),
        scratch_shapes=[pltpu.VMEM((g_pages * PAGE, MLA_LORA), BF16),
                        pltpu.VMEM((MLA_ROPE, g_pages * PAGE), BF16)] + _flash_scratch(DEC_CHAINS, 128, MLA_LORA),
    )
    return pl.pallas_call(
        functools.partial(_mla_dec_kernel, n_pages=n_pages, n_new=n_new),
        grid_spec=grid_spec,
        out_shape=jax.ShapeDtypeStruct((n_b, n_new, MLA_HEADS, MLA_LORA), F32),
        compiler_params=_params(("arbitrary", "arbitrary")),
        name="mla_attn_sample",
    )(pt, q_s, knew, *([cache_ckv] * g_pages), *([cache_kpe_t] * g_pages))


def _mla_up_kernel(o_ref, wv_ref, u_ref):
    for h in range(MLA_HEADS):
        oh = o_ref[:, h * MLA_LORA:(h + 1) * MLA_LORA].astype(BF16)
        u_ref[:, h * MLA_V:(h + 1) * MLA_V] = _dot(oh, wv_ref[h])


def _mla_up(o_lat, wv):
    m = o_lat.shape[0]
    return pl.pallas_call(
        _mla_up_kernel,
        grid=(m // TM,),
        in_specs=[pl.BlockSpec((TM, MLA_HEADS * MLA_LORA), lambda i: (i, 0)), _resident(wv.shape)],
        out_specs=pl.BlockSpec((TM, D_MODEL), lambda i: (i, 0)),
        out_shape=jax.ShapeDtypeStruct((m, D_MODEL), F32),
        compiler_params=_params(("arbitrary",)),
        name="mla_up",
    )(o_lat, wv)


def _head_rms(x, g, scale):
    return x * (lax.rsqrt(jnp.mean(x * x, axis=-1, keepdims=True) + NORM_EPS) * scale) * g


def _fox_proj_kernel(x_ref, gpre_ref, w_ref, bf_ref, qn_ref, kn_ref,
                     q_ref, k_ref, v_ref, kb_ref, vb_ref, lf_ref, c_ref, gate_ref, carry_ref, *, tiles_per_seq):
    i = pl.program_id(0)
    hb = _rms(x_ref[...], gpre_ref[...]).astype(BF16)
    q = _dot(hb, w_ref[:, 0:2048])
    for h in range(FOX_HEADS):
        sl = slice(h * FOX_HD, (h + 1) * FOX_HD)
        q_ref[:, sl] = _head_rms(q[:, sl], qn_ref[...], FOX_SCALE).astype(BF16)
    k = _dot(hb, w_ref[:, 2048:2560])
    for h in range(FOX_KVH):
        sl = slice(h * FOX_HD, (h + 1) * FOX_HD)
        kh = _head_rms(k[:, sl], kn_ref[...], 1.0)
        k_ref[:, sl] = kh
        kb_ref[:, sl] = kh.astype(BF16)
    v = _dot(hb, w_ref[:, 2560:3072])
    v_ref[...] = v
    vb_ref[...] = v.astype(BF16)
    fl = _dot(hb, w_ref[:, 3072:3200]) + bf_ref[...]
    lf = -_softplus(-fl)
    lf_ref[...] = lf[:, 0:FOX_HEADS]

    @pl.when(i % tiles_per_seq == 0)
    def _():
        carry_ref[...] = jnp.zeros(carry_ref.shape, F32)

    r = lax.broadcasted_iota(jnp.int32, (TM, TM), 0)
    c = lax.broadcasted_iota(jnp.int32, (TM, TM), 1)
    tri = (r >= c).astype(F32)
    cum = jnp.dot(tri, lf, preferred_element_type=F32, precision=HI) + carry_ref[...]
    c_ref[...] = cum[:, 0:FOX_HEADS]
    carry_ref[...] = cum[TM - 1:TM, :]
    gate_ref[...] = _dot(hb, w_ref[:, 3200:5248])


def _fox_proj(x, gpre, w_main, bf, qn, kn, tiles_per_seq):
    m = x.shape[0]
    row = lambda w: pl.BlockSpec((TM, w), lambda i: (i, 0))
    kvw = FOX_KVH * FOX_HD
    return pl.pallas_call(
        functools.partial(_fox_proj_kernel, tiles_per_seq=tiles_per_seq),
        grid=(m // TM,),
        in_specs=[row(D_MODEL), _resident((1, D_MODEL)), _resident(w_main.shape), _resident((1, 128)),
                  _resident((1, FOX_HD)), _resident((1, FOX_HD))],
        out_specs=[row(D_MODEL), row(kvw), row(kvw), row(kvw), row(kvw), row(FOX_HEADS), row(FOX_HEADS),
                   row(D_MODEL)],
        out_shape=[jax.ShapeDtypeStruct((m, D_MODEL), BF16), jax.ShapeDtypeStruct((m, kvw), F32),
                   jax.ShapeDtypeStruct((m, kvw), F32), jax.ShapeDtypeStruct((m, kvw), BF16),
                   jax.ShapeDtypeStruct((m, kvw), BF16), jax.ShapeDtypeStruct((m, FOX_HEADS), F32),
                   jax.ShapeDtypeStruct((m, FOX_HEADS), F32), jax.ShapeDtypeStruct((m, D_MODEL), F32)],
        scratch_shapes=[pltpu.VMEM((1, 128), F32)],
        compiler_params=_params(("arbitrary",)),
        name="fox_proj",
    )(x, gpre, w_main, bf, qn, kn)


def _fox_attn_prompt_kernel(q_ref, k_ref, v_ref, c_ref, o_ref, acc_ref, m_ref, l_ref):
    qi = pl.program_id(2)
    half = FOX_GROUP // 2
    rows = half * TQ
    qs = [jnp.concatenate([q_ref[:, g * FOX_HD:(g + 1) * FOX_HD] for g in range(c * half, (c + 1) * half)], axis=0)
          for c in range(2)]
    _flash_init(m_ref, l_ref, acc_ref)

    def qk(kb):
        k = k_ref[pl.ds(pl.multiple_of(kb * TQ, TQ), TQ), :]
        return tuple(_dot_nt(qs[c], k) for c in range(2))

    def update(kb, raw, masked):
        start = pl.multiple_of(kb * TQ, TQ)
        v = v_ref[pl.ds(start, TQ), :]
        cb = c_ref[0, 0, :, pl.ds(start, TQ)]
        scores = {c: (s.reshape(half, TQ, TQ) - cb[c * half:(c + 1) * half, None, :]).reshape(rows, TQ)
                  for c, s in enumerate(raw)}
        if masked:
            tok = lax.broadcasted_iota(jnp.int32, (rows, TQ), 0) & (TQ - 1)
            key = lax.broadcasted_iota(jnp.int32, (rows, TQ), 1)
            scores = {c: jnp.where(key <= tok, s, NEG) for c, s in scores.items()}
        _flash_update(scores, {0: v, 1: v}, m_ref, l_ref, acc_ref, mxu_row_sum=True)

    def body(kb, raw):
        nxt = qk(kb + 1)
        update(kb, raw, False)
        return nxt

    update(qi, lax.fori_loop(0, qi, body, qk(0)), True)
    for c in range(2):
        o = acc_ref[c] / l_ref[c]
        for gg in range(half):
            g = c * half + gg
            o_ref[:, g * FOX_HD:(g + 1) * FOX_HD] = o[gg * TQ:(gg + 1) * TQ, :]


def _fox_attn_prompt(q, kb, vb, c_t, n_batch, seq):
    nq = seq // TQ
    rows = FOX_GROUP // 2 * TQ
    gw = FOX_GROUP * FOX_HD
    return pl.pallas_call(
        _fox_attn_prompt_kernel,
        grid=(n_batch, FOX_KVH, nq),
        in_specs=[pl.BlockSpec((TQ, gw), lambda b, h, qi: (b * nq + qi, h)),
                  pl.BlockSpec((seq, FOX_HD), lambda b, h, qi: (b, h)),
                  pl.BlockSpec((seq, FOX_HD), lambda b, h, qi: (b, h)),
                  pl.BlockSpec((1, 1, FOX_GROUP, seq), lambda b, h, qi: (b, h, 0, 0))],
        out_specs=pl.BlockSpec((TQ, gw), lambda b, h, qi: (b * nq + qi, h)),
        out_shape=jax.ShapeDtypeStruct((n_batch * seq, D_MODEL), F32),
        scratch_shapes=_flash_scratch(2, rows, FOX_HD),
        compiler_params=_params(("arbitrary", "arbitrary", "arbitrary")),
        name="fox_attn_prompt",
    )(q, kb, vb, c_t)


def _lane_cumsum(x):
    lane = lax.broadcasted_iota(jnp.int32, x.shape, 1)
    d = 1
    while d < x.shape[1]:
        x = x + jnp.where(lane >= d, pltpu.roll(x, d, 1), 0.0)
        d *= 2
    return x


def _fox_dec_kernel(pt_ref, q_ref, knew_ref, vnew_ref, lfnew_ref, *rest, n_pages, n_new):
    g_pages = PAGES_PER_STEP
    k_refs = rest[:g_pages]
    v_refs = rest[g_pages:2 * g_pages]
    lf_refs = rest[2 * g_pages:3 * g_pages]
    o_ref = rest[3 * g_pages]
    kbuf, vbuf, carry_ref, acc_ref, m_ref, l_ref = rest[3 * g_pages + 1:]
    g = pl.program_id(1)
    kvw = FOX_KVH * FOX_HD
    part = g_pages * PAGE // DEC_CHAINS

    @pl.when(g == 0)
    def _():
        _flash_init(m_ref, l_ref, acc_ref)
        carry_ref[...] = jnp.zeros(carry_ref.shape, F32)

    for i in range(g_pages):
        for h in range(FOX_KVH):
            sl = slice(h * FOX_HD, (h + 1) * FOX_HD)
            kbuf[i * PAGE:(i + 1) * PAGE, sl] = k_refs[i][pl.ds(h, PAGE, stride=FOX_KVH), :].astype(BF16)
            vbuf[i * PAGE:(i + 1) * PAGE, sl] = v_refs[i][pl.ds(h, PAGE, stride=FOX_KVH), :].astype(BF16)

    lf_t = jnp.concatenate([lf_refs[i][...] for i in range(g_pages)], axis=0)
    cum = _lane_cumsum(lf_t)
    tot = jnp.broadcast_to(cum[:, PAGE - 1:PAGE], cum.shape)
    inc = tot
    d = FOX_HEADS
    while d < g_pages * FOX_HEADS:
        inc = inc + jnp.concatenate([jnp.zeros((d, PAGE), F32), inc[:-d, :]], axis=0)
        d *= 2
    carry = carry_ref[...]
    c_keys = cum + (inc - tot) + jnp.concatenate([carry] * g_pages, axis=0)
    carry_ref[...] = carry + inc[(g_pages - 1) * FOX_HEADS:, :]

    q = q_ref[0]
    row_kvh = (lax.broadcasted_iota(jnp.int32, (128, kvw), 0) % FOX_HEADS) // FOX_GROUP
    col_kvh = lax.broadcasted_iota(jnp.int32, (128, kvw), 1) // FOX_HD
    own = row_kvh == col_kvh
    qbd = jnp.where(own, jnp.concatenate([q] * FOX_KVH, axis=1), jnp.zeros((), BF16))
    bias = jnp.concatenate(
        [jnp.concatenate([c_keys[i * FOX_HEADS:(i + 1) * FOX_HEADS, :]] * (128 // FOX_HEADS), axis=0)
         for i in range(g_pages)], axis=1)
    halves = {c: slice(c * part, (c + 1) * part) for c in range(DEC_CHAINS)}
    scores = {c: _dot_nt(qbd, kbuf[ks, :]) for c, ks in halves.items()}
    scores = {c: scores[c] - bias[:, ks] for c, ks in halves.items()}
    _flash_update(scores, {c: vbuf[ks, :] for c, ks in halves.items()}, m_ref, l_ref, acc_ref)

    @pl.when(g == n_pages // g_pages - 1)
    def _():
        c_new = carry_ref[...] + _lane_cumsum(lfnew_ref[0])
        bias_n = jnp.concatenate([c_new] * (128 // FOX_HEADS), axis=0)[:, 0:16]
        s = _dot_nt(qbd, knew_ref[0]) - bias_n
        tok = lax.broadcasted_iota(jnp.int32, s.shape, 0) // FOX_HEADS
        key = lax.broadcasted_iota(jnp.int32, s.shape, 1)
        s = jnp.where((key <= tok) & (key < n_new), s, NEG)
        _flash_update({0: s}, {0: vnew_ref[0]}, m_ref, l_ref, acc_ref)
        o = jnp.where(own, _flash_merge(m_ref, l_ref, acc_ref), 0.0)
        o = o[:, 0:128] + o[:, 128:256] + o[:, 256:384] + o[:, 384:512]
        o_ref[0] = o[0:n_new * FOX_HEADS, :].reshape(n_new, FOX_HEADS, FOX_HD)


def _fox_attn_sample(page_table, q_s, knew, vnew, lfnew_t, cache_k, cache_v, cache_lf_t, layer, n_new):
    n_b, n_pages = page_table.shape
    g_pages = PAGES_PER_STEP
    pt = page_table.reshape(-1)
    kvw = FOX_KVH * FOX_HD

    def page_spec(rows, width, i):
        return pl.BlockSpec((None, None, rows, width),
                            lambda b, g, pt_ref: (layer, pt_ref[b * n_pages + g * g_pages + i], 0, 0))

    per_b = lambda shape: pl.BlockSpec((1,) + shape, lambda b, g, pt_ref: (b, 0, 0))
    grid_spec = pltpu.PrefetchScalarGridSpec(
        num_scalar_prefetch=1,
        grid=(n_b, n_pages // g_pages),
        in_specs=[per_b((128, FOX_HD)), per_b((16, kvw)), per_b((16, kvw)), per_b((FOX_HEADS, 128))]
                 + [page_spec(PAGE * FOX_KVH, FOX_HD, i) for i in range(g_pages)]
                 + [page_spec(PAGE * FOX_KVH, FOX_HD, i) for i in range(g_pages)]
                 + [page_spec(FOX_HEADS, PAGE, i) for i in range(g_pages)],
        out_specs=pl.BlockSpec((1, n_new, FOX_HEADS, FOX_HD), lambda b, g, pt_ref: (b, 0, 0, 0)),
        scratch_shapes=[pltpu.VMEM((g_pages * PAGE, kvw), BF16), pltpu.VMEM((g_pages * PAGE, kvw), BF16),
                        pltpu.VMEM((FOX_HEADS, 128), F32)] + _flash_scratch(DEC_CHAINS, 128, kvw),
    )
    return pl.pallas_call(
        functools.partial(_fox_dec_kernel, n_pages=n_pages, n_new=n_new),
        grid_spec=grid_spec,
        out_shape=jax.ShapeDtypeStruct((n_b, n_new, FOX_HEADS, FOX_HD), F32),
        compiler_params=_params(("arbitrary", "arbitrary")),
        name="fox_attn_sample",
    )(pt, q_s, knew, vnew, lfnew_t, *([cache_k] * g_pages), *([cache_v] * g_pages), *([cache_lf_t] * g_pages))


def _rms_kernel(x_ref, g_ref, h_ref):
    h_ref[...] = _rms(x_ref[...], g_ref[...])


def _rms_norm(x, g):
    m = x.shape[0]
    row = pl.BlockSpec((TM, D_MODEL), lambda i: (i, 0))
    return pl.pallas_call(
        _rms_kernel, grid=(m // TM,), in_specs=[row, _resident((1, D_MODEL))], out_specs=row,
        out_shape=jax.ShapeDtypeStruct((m, D_MODEL), F32), compiler_params=_params(("arbitrary",)),
        name="rms_norm",
    )(x, g)


def _rwkv_mix_kernel(h_ref, hp_ref, mu_ref, w_ref, o_ref):
    h = h_ref[...]
    xm = h + (hp_ref[...] - h) * mu_ref[0]
    o_ref[0] = _dot(xm.astype(BF16), w_ref[0])


def _rwkv_mix_proj(h, hp, mu4, w4):
    m = h.shape[0]
    row = pl.BlockSpec((TM, D_MODEL), lambda p, i: (i, 0))
    return pl.pallas_call(
        _rwkv_mix_kernel,
        grid=(4, m // TM),
        in_specs=[row, row, pl.BlockSpec((1, 1, D_MODEL), lambda p, i: (p, 0, 0)),
                  pl.BlockSpec((1, D_MODEL, D_MODEL), lambda p, i: (p, 0, 0))],
        out_specs=pl.BlockSpec((1, TM, D_MODEL), lambda p, i: (p, i, 0)),
        out_shape=jax.ShapeDtypeStruct((4, m, D_MODEL), F32),
        compiler_params=_params(("arbitrary", "arbitrary")),
        name="rwkv_mix_proj",
    )(h, hp, mu4, w4)


def _split2(x):
    hi = x.astype(BF16)
    return hi, (x - hi.astype(F32)).astype(BF16)


def _head_sums(x):
    r = lax.broadcasted_iota(jnp.int32, (256, 128), 0)
    c = lax.broadcasted_iota(jnp.int32, (256, 128), 1)
    ones2 = ((r & 127) // RWKV_HD == c // RWKV_HD).astype(BF16)
    out = []
    for j in range(x.shape[1] // 128):
        hi, lo = _split2(x[:, j * 128:(j + 1) * 128])
        out.append(_dot(jnp.concatenate([hi, lo], axis=1), ones2))
    return jnp.concatenate(out, axis=1)


def _rwkv_post_kernel(h_ref, hp_ref, r_ref, k_ref, v_ref, muw_ref, mua_ref, w0_ref, w1_ref, w2_ref,
                      a0_ref, a1_ref, a2_ref, kk_ref, ka_ref, rk_ref, *outs, chunked):
    h = h_ref[...]
    dh = hp_ref[...] - h
    xw = (h + dh * muw_ref[...]).astype(BF16)
    xa = (h + dh * mua_ref[...]).astype(BF16)
    wl = w0_ref[...] + _dot(jnp.tanh(_dot(xw, w1_ref[...])).astype(BF16), w2_ref[...])
    lw = -jnp.exp(-_softplus(-wl) - 0.5)
    asig = _sigmoid(a0_ref[...] + _dot(_dot(xa, a1_ref[...]).astype(BF16), a2_ref[...]))
    k = k_ref[0]
    k4 = k * (1.0 + (asig - 1.0) * ka_ref[...])
    kr = k * kk_ref[...]
    kk = kr * lax.rsqrt(jnp.maximum(_head_sums(kr * kr), 1e-24))
    a_vec = -kk
    b_vec = kk * asig
    if not chunked:
        w_ref, k4_ref, a_ref, b_ref = outs
        w_ref[...] = jnp.exp(lw)
        k4_ref[...] = k4
        a_ref[...] = a_vec
        b_ref[...] = b_vec
        return
    x1_ref, x2_ref, bonus_ref, pin_ref = outs
    r = r_ref[0]
    ri = lax.broadcasted_iota(jnp.int32, (TM, TM), 0)
    ci = lax.broadcasted_iota(jnp.int32, (TM, TM), 1)
    tri = ((ri >= ci) & (ri // RWKV_CHUNK == ci // RWKV_CHUNK)).astype(BF16)
    hi = lw.astype(BF16)
    mid, lo = _split2(lw - hi.astype(F32))
    cl =_dot(jnp.concatenate([tri, tri, tri], axis=1), jnp.concatenate([hi, mid, lo], axis=0))
    p_in = jnp.exp(cl)
    p_inv = jnp.exp(-cl)
    at = (a_vec * jnp.exp(cl - lw)).astype(BF16)
    rt = (r * p_in).astype(BF16)
    bt = (b_vec * p_inv).astype(BF16)
    kt = (k4 * p_inv).astype(BF16)
    n = RWKV_CHUNK
    for c in range(TM // n):
        x1_ref[2 * n * c:2 * n * c + n, :] = at[n * c:n * (c + 1), :]
        x1_ref[2 * n * c + n:2 * n * (c + 1), :] = rt[n * c:n * (c + 1), :]
        x2_ref[2 * n * c:2 * n * c + n, :] = bt[n * c:n * (c + 1), :]
        x2_ref[2 * n * c + n:2 * n * (c + 1), :] = kt[n * c:n * (c + 1), :]
    bonus_ref[...] = _head_sums(r * k4 * rk_ref[...]) * v_ref[0]
    pin_ref[...] = p_in


def _rwkv_post(h, hp, rkvg, vecs, w1, w2, a1, a2, row0, n_rows, chunked):
    t0 = row0 // TM
    row = pl.BlockSpec((TM, D_MODEL), lambda i: (i + t0, 0))
    proj = lambda p: pl.BlockSpec((1, TM, D_MODEL), lambda i: (p, i + t0, 0))
    out_row = pl.BlockSpec((TM, D_MODEL), lambda i: (i, 0))
    vec = _resident((1, D_MODEL))
    muw, mua, w0, a0, kk, ka, rk = vecs
    if chunked:
        stacked = pl.BlockSpec((2 * TM, D_MODEL), lambda i: (i, 0))
        out_specs = [stacked, stacked, out_row, out_row]
        out_shape = [jax.ShapeDtypeStruct((2 * n_rows, D_MODEL), BF16)] * 2 + \
                    [jax.ShapeDtypeStruct((n_rows, D_MODEL), F32)] * 2
    else:
        out_specs = [out_row] * 4
        out_shape = [jax.ShapeDtypeStruct((n_rows, D_MODEL), F32)] * 4
    return pl.pallas_call(
        functools.partial(_rwkv_post_kernel, chunked=chunked),
        grid=(n_rows // TM,),
        in_specs=[row, row, proj(0), proj(1), proj(2), vec, vec, vec, _resident(w1.shape), _resident(w2.shape),
                  vec, _resident(a1.shape), _resident(a2.shape), vec, vec, vec],
        out_specs=out_specs,
        out_shape=out_shape,
        compiler_params=_params(("arbitrary",)),
        name="rwkv_post_chunked" if chunked else "rwkv_post_steps",
    )(h, hp, rkvg, rkvg, rkvg, muw, mua, w0, w1, w2, a0, a1, a2, kk, ka, rk)


def _rwkv_chunks(states, x1s, x2s, vs):
    heads = range(len(vs))
    n = vs[0].shape[0]
    row = lax.broadcasted_iota(jnp.int32, (2 * n, 2 * n), 0)
    col = lax.broadcasted_iota(jnp.int32, (2 * n, 2 * n), 1) & (n - 1)
    keep = (row > col) & ((row < n) | (row - n >= col))
    zeros = jnp.zeros((n, RWKV_HD), F32)
    gs = [jnp.where(keep, _dot_nt(x1s[h], x2s[h]), 0.0).astype(BF16) for h in heads]
    zs = [_dot_nt(x1s[h], states[h].astype(BF16)) for h in heads]
    us = [zs[h][0:n] + _dot(gs[h][0:n], jnp.concatenate([zeros, vs[h]], axis=0).astype(BF16)) for h in heads]
    nks = [gs[h][0:n, 0:n] for h in heads]
    levels = int(math.log2(n))
    for lvl in range(levels):
        us = [us[h] + _dot(nks[h], us[h].astype(BF16)) for h in heads]
        if lvl + 1 < levels:
            nks = [_dot(nks[h], nks[h]).astype(BF16) for h in heads]
    uvs = [jnp.concatenate([us[h], vs[h]], axis=0).astype(BF16) for h in heads]
    ys = [zs[h][n:2 * n] + _dot(gs[h][n:2 * n], uvs[h]) for h in heads]
    new_states = [states[h] + lax.dot_general(uvs[h], x2s[h], TN, preferred_element_type=F32) for h in heads]
    return new_states, ys


def _rwkv_scan_kernel(x1_ref, x2_ref, v_ref, bonus_ref, pin_ref, lnw_ref, lnb_ref,
                      u_ref, sout_ref, state_ref, *, heads, n_chunks):
    c = pl.program_id(2)

    @pl.when(c == 0)
    def _():
        state_ref[...] = jnp.zeros(state_ref.shape, F32)

    sls = [slice(h * RWKV_HD, (h + 1) * RWKV_HD) for h in range(heads)]
    new_states, ys = _rwkv_chunks([state_ref[h] for h in range(heads)], [x1_ref[:, sl] for sl in sls],
                                  [x2_ref[:, sl] for sl in sls], [v_ref[0, :, sl] for sl in sls])
    outs = []
    for h, sl in enumerate(sls):
        state_ref[h] = new_states[h] * pin_ref[7:8, sl]
        y = ys[h]
        mean = jnp.mean(y, axis=-1, keepdims=True)
        var = jnp.mean(jnp.square(y - mean), axis=-1, keepdims=True)
        outs.append((y - mean) * lax.rsqrt(var + RWKV_LNX_EPS) * lnw_ref[:, sl] + lnb_ref[:, sl] + bonus_ref[:, sl])
    u_ref[...] = jnp.concatenate(outs, axis=-1)

    @pl.when(c == n_chunks - 1)
    def _():
        sout_ref[0] = state_ref[...]


def _rwkv_scan(x1, x2, rkvg, bonus, pin, lnw, lnb, n_seq, seq):
    n = RWKV_CHUNK
    heads = RWKV_SCAN_HEADS
    n_chunks = seq // n
    w = heads * RWKV_HD
    tok = lambda rows, scale: pl.BlockSpec((rows, w), lambda s, hg, c: (scale * (s * n_chunks + c), hg))
    last8 = pl.BlockSpec((8, w), lambda s, hg, c: ((s * n_chunks + c + 1) * (n // 8) - 1, hg))
    vec = pl.BlockSpec((1, w), lambda s, hg, c: (0, hg))
    return pl.pallas_call(
        functools.partial(_rwkv_scan_kernel, heads=heads, n_chunks=n_chunks),
        grid=(n_seq, RWKV_HEADS // heads, n_chunks),
        in_specs=[tok(2 * n, 1), tok(2 * n, 1),
                  pl.BlockSpec((1, n, w), lambda s, hg, c: (2, s * n_chunks + c, hg)),
                  tok(n, 1), last8, vec, vec],
        out_specs=[tok(n, 1), pl.BlockSpec((1, heads, RWKV_HD, RWKV_HD), lambda s, hg, c: (s, hg, 0, 0))],
        out_shape=[jax.ShapeDtypeStruct((n_seq * seq, D_MODEL), F32),
                   jax.ShapeDtypeStruct((n_seq, RWKV_HEADS, RWKV_HD, RWKV_HD), F32)],
        scratch_shapes=[pltpu.VMEM((heads, RWKV_HD, RWKV_HD), F32)],
        compiler_params=_params(("arbitrary", "arbitrary", "arbitrary")),
        name="rwkv_scan",
    )(x1, x2, rkvg, bonus, pin, lnw, lnb)


def _rwkv_steps_kernel(r_ref, w_ref, k_ref, v_ref, a_ref, b_ref, s0_ref, rk_ref, lnw_ref, lnb_ref,
                       u_ref, sout_ref, y_ref, *, heads, n_new):
    for h in range(heads):
        hs = slice(h * RWKV_HD, (h + 1) * RWKV_HD)

        def body(vi, carry):
            s = s0_ref[h, vi]
            for t in range(n_new):
                sa = jnp.sum(s * a_ref[t, hs, :], axis=0, keepdims=True)
                vv = v_ref[t, pl.ds(h * RWKV_HD + vi, 1), :]
                s = s * w_ref[t, hs, :] + sa * b_ref[t, hs, :] + vv * k_ref[t, hs, :]
                y_ref[t, pl.ds(h * RWKV_HD + vi, 1), :] = jnp.sum(s * r_ref[t, hs, :], axis=0, keepdims=True)
            sout_ref[h, vi] = s
            return carry

        lax.fori_loop(0, RWKV_HD, body, 0)
        for t in range(n_new):
            y = y_ref[t, hs, :]
            mean = jnp.mean(y, axis=0, keepdims=True)
            var = jnp.mean(jnp.square(y - mean), axis=0, keepdims=True)
            yn = (y - mean) * lax.rsqrt(var + RWKV_LNX_EPS) * lnw_ref[hs, :] + lnb_ref[hs, :]
            bonus = jnp.sum(r_ref[t, hs, :] * k_ref[t, hs, :] * rk_ref[hs, :], axis=0, keepdims=True)
            u_ref[t, hs, :] = yn + bonus * v_ref[t, hs, :]


def _rwkv_steps(r, w, k, v, a, b, s0_t, rk, lnw, lnb):
    n_new, _, n_db = r.shape
    heads = 2
    rows = heads * RWKV_HD
    tok = pl.BlockSpec((n_new, rows, n_db), lambda g: (0, g, 0))
    vec = pl.BlockSpec((rows, n_db), lambda g: (g, 0))
    st = pl.BlockSpec((heads, RWKV_HD, RWKV_HD, n_db), lambda g: (g, 0, 0, 0))
    return pl.pallas_call(
        functools.partial(_rwkv_steps_kernel, heads=heads, n_new=n_new),
        grid=(RWKV_HEADS // heads,),
        in_specs=[tok] * 6 + [st, vec, vec, vec],
        out_specs=[tok, st],
        out_shape=[jax.ShapeDtypeStruct((n_new, D_MODEL, n_db), F32),
                   jax.ShapeDtypeStruct((RWKV_HEADS, RWKV_HD, RWKV_HD, n_db), F32)],
        scratch_shapes=[pltpu.VMEM((n_new, rows, n_db), F32)],
        compiler_params=_params(("arbitrary",)),
        name="rwkv_steps",
    )(r, w, k, v, a, b, s0_t, rk, lnw, lnb)


def _out_proj_kernel(up_ref, us_ref, g_ref, x_ref, w_ref, gp_ref, o_ref, *, prompt_tiles):
    g = g_ref[...]
    u = jnp.where(pl.program_id(0) < prompt_tiles, up_ref[...], us_ref[...])
    a = (u * (g * _sigmoid(g))).astype(BF16)
    o_ref[...] = x_ref[...] + _rms(_dot(a, w_ref[...]), gp_ref[...])


def _out_proj(u_prompt, u_sample, gate, x, w_o, gpost):
    m = x.shape[0]
    pt = u_prompt.shape[0] // TM
    row = pl.BlockSpec((TM, D_MODEL), lambda i: (i, 0))
    if gate.ndim == 3:
        gate_spec = pl.BlockSpec((None, TM, D_MODEL), lambda i: (gate.shape[0] - 1, i, 0))
    else:
        gate_spec = row
    return pl.pallas_call(
        functools.partial(_out_proj_kernel, prompt_tiles=pt), grid=(m // TM,),
        in_specs=[pl.BlockSpec((TM, D_MODEL), lambda i: (jnp.minimum(i, pt - 1), 0)),
                  pl.BlockSpec((TM, D_MODEL), lambda i: (jnp.maximum(i - pt, 0), 0)),
                  gate_spec, row, _resident(w_o.shape), _resident((1, D_MODEL))], out_specs=row,
        out_shape=jax.ShapeDtypeStruct((m, D_MODEL), F32), compiler_params=_params(("arbitrary",)),
        name="out_proj",
    )(u_prompt, u_sample, gate, x, w_o, gpost)


def _mla_layer(x, dims, page_table, cache_ckv, cache_kpe, layer, tables, gpre, w_in, q_norm, kv_norm, w_uq, w_ukv):
    n_b, seq, n_db, n_new = dims
    mp = n_b * seq
    w_main = jnp.concatenate(
        [w_in[:, 0:1024], w_in[:, 1024:1088],
         -w_in[:, 1056:1088], w_in[:, 1024:1056], w_in[:, 1088:]], axis=1).astype(BF16)
    uq3 = w_uq.reshape(MLA_LORA, MLA_HEADS, MLA_NOPE + MLA_ROPE)
    wuqn = uq3[:, :, 0:MLA_NOPE].reshape(MLA_LORA, -1).astype(BF16)
    pe = uq3[:, :, MLA_NOPE:]
    wuqp = pe.reshape(MLA_LORA, -1).astype(BF16)
    wuqr = jnp.concatenate([-pe[:, :, 32:], pe[:, :, :32]], axis=-1).reshape(MLA_LORA, -1).astype(BF16)
    wk = jnp.transpose(w_ukv[:, :, 0:MLA_NOPE], (1, 2, 0)).astype(BF16)
    wv = jnp.transpose(w_ukv[:, :, MLA_NOPE:], (1, 0, 2)).astype(BF16)
    gate, ckv, kpe, kcat, qfull = _mla_proj(x, gpre, w_main, q_norm, kv_norm, wuqn, wuqp, wuqr, wk, *tables)
    u_p = _mla_attn_prompt(qfull, kcat, wv, n_b, seq)
    q_s = jnp.transpose(qfull[:, mp:, :], (1, 0, 2)).reshape(n_db, n_new * MLA_HEADS, MLA_KW)
    q_s = jnp.pad(q_s, ((0, 0), (0, 128 - n_new * MLA_HEADS), (0, 0)))
    knew = jnp.pad(kcat[mp:].reshape(n_db, n_new, MLA_KW), ((0, 0), (0, 16 - n_new), (0, 0)))
    o_lat = _mla_attn_sample(page_table, q_s, knew, cache_ckv, jnp.swapaxes(cache_kpe, 2, 3), layer, n_new)
    u_s = _mla_up(o_lat.reshape(n_db * n_new, MLA_HEADS * MLA_LORA), wv)
    return (u_p, u_s), gate, ckv, kpe


def _fox_layer(x, dims, page_table, cache_k, cache_v, cache_lf, layer, gpre, w_in, b_f, q_norm, k_norm):
    n_b, seq, n_db, n_new = dims
    mp = n_b * seq
    kvw = FOX_KVH * FOX_HD
    nq = FOX_HEADS * FOX_HD
    w_main = jnp.concatenate(
        [w_in[:, 0:nq + 2 * kvw], w_in[:, nq + 2 * kvw:nq + 2 * kvw + FOX_HEADS],
         jnp.zeros((D_MODEL, 128 - FOX_HEADS), F32), w_in[:, nq + 2 * kvw + FOX_HEADS:]], axis=1).astype(BF16)
    bf = jnp.pad(b_f, (0, 128 - FOX_HEADS)).reshape(1, 128)
    q, k, v, kb, vb, lf, c, gate = _fox_proj(x, gpre, w_main, bf, q_norm.reshape(1, -1), k_norm.reshape(1, -1),
                                             seq // TM)
    c_t = jnp.transpose(c[:mp].reshape(n_b, seq, FOX_HEADS), (0, 2, 1)).reshape(n_b, FOX_KVH, FOX_GROUP, seq)
    u_p = _fox_attn_prompt(q, kb, vb, c_t, n_b, seq)
    q_s = jnp.pad(q[mp:].reshape(n_db, n_new * FOX_HEADS, FOX_HD), ((0, 0), (0, 128 - n_new * FOX_HEADS), (0, 0)))
    knew = jnp.pad(kb[mp:].reshape(n_db, n_new, kvw), ((0, 0), (0, 16 - n_new), (0, 0)))
    vnew = jnp.pad(vb[mp:].reshape(n_db, n_new, kvw), ((0, 0), (0, 16 - n_new), (0, 0)))
    lfnew_t = jnp.pad(jnp.transpose(lf[mp:].reshape(n_db, n_new, FOX_HEADS), (0, 2, 1)),
                      ((0, 0), (0, 0), (0, 128 - n_new)))
    n_pool = cache_k.shape[1]
    o_s = _fox_attn_sample(page_table, q_s, knew, vnew, lfnew_t,
                           cache_k.reshape(cache_k.shape[0], n_pool, PAGE * FOX_KVH, FOX_HD),
                           cache_v.reshape(cache_v.shape[0], n_pool, PAGE * FOX_KVH, FOX_HD),
                           jnp.swapaxes(cache_lf, 2, 3), layer, n_new)
    return (u_p, o_s.reshape(n_db * n_new, D_MODEL)), gate, k, v, lf


def _rwkv_layer(x, dims, shift0, s0, gpre, mu, w_rkvg, w0, w1, w2, a0, a1, a2, k_k, k_a, r_k, lnx_w, lnx_b):
    n_b, seq, n_db, n_new = dims
    mp = n_b * seq
    vec = lambda a: a.reshape(1, D_MODEL)
    h = _rms_norm(x, gpre)
    h_p = h[:mp].reshape(n_b, seq, D_MODEL)
    h_s = h[mp:].reshape(n_db, n_new, D_MODEL)
    hp = jnp.concatenate(
        [jnp.concatenate([jnp.zeros((n_b, 1, D_MODEL), F32), h_p[:, :-1]], axis=1).reshape(mp, D_MODEL),
         jnp.concatenate([shift0[:, None, :], h_s[:, :-1]], axis=1).reshape(n_db * n_new, D_MODEL)], axis=0)
    rkvg = _rwkv_mix_proj(h, hp, mu[0:4].reshape(4, 1, D_MODEL), w_rkvg.astype(BF16))
    lora = w1.shape[1]
    w1p = jnp.pad(w1, ((0, 0), (0, RWKV_LORA_PAD - lora))).astype(BF16)
    w2p = jnp.pad(w2, ((0, RWKV_LORA_PAD - lora), (0, 0))).astype(BF16)
    a1p = jnp.pad(a1, ((0, 0), (0, RWKV_LORA_PAD - lora))).astype(BF16)
    a2p = jnp.pad(a2, ((0, RWKV_LORA_PAD - lora), (0, 0))).astype(BF16)
    vecs = (vec(mu[4]), vec(mu[5]), vec(w0), vec(a0), vec(k_k), vec(k_a), vec(r_k))
    x1, x2, bonus, pin = _rwkv_post(h, hp, rkvg, vecs, w1p, w2p, a1p, a2p, 0, mp, True)
    u_p, s_p = _rwkv_scan(x1, x2, rkvg, bonus, pin, vec(lnx_w), vec(lnx_b), n_b, seq)
    w_s, k_s, a_s, b_s = _rwkv_post(h, hp, rkvg, vecs, w1p, w2p, a1p, a2p, mp, n_db * n_new, False)
    to_lanes = lambda a: jnp.transpose(a.reshape(n_db, n_new, D_MODEL), (1, 2, 0))
    col = lambda a: jnp.broadcast_to(a.reshape(D_MODEL, 1), (D_MODEL, n_db))
    u_t, s_t = _rwkv_steps(to_lanes(rkvg[0, mp:]), to_lanes(w_s), to_lanes(k_s), to_lanes(rkvg[2, mp:]),
                           to_lanes(a_s), to_lanes(b_s), jnp.transpose(s0, (1, 2, 3, 0)),
                           col(r_k), col(lnx_w), col(lnx_b))
    u_s = jnp.transpose(u_t, (2, 0, 1)).reshape(n_db * n_new, D_MODEL)
    return (u_p, u_s), rkvg, s_p, jnp.transpose(s_t, (3, 0, 1, 2)), h_p[:, -1], h_s[:, -1]


def _rope_tables(n_b, seq, n_db, n_new, n_past):
    half = MLA_ROPE // 2
    inv = ROPE_THETA ** (-jnp.arange(half, dtype=F32) / half)
    pos = jnp.concatenate([jnp.tile(jnp.arange(seq, dtype=F32), n_b),
                           jnp.tile(n_past + jnp.arange(n_new, dtype=F32), n_db)])
    ang = pos[:, None] * inv[None, :]
    return jnp.tile(jnp.cos(ang), (1, 4)), jnp.tile(jnp.sin(ang), (1, 4))


def kernel(x_prompt, x_sample, cache_mla_ckv, cache_mla_kpe, cache_fox_k, cache_fox_v, cache_fox_logf,
           state_rwkv_wkv, state_rwkv_shift, page_table, norm_pre, norm_post,
           mla_w_in, mla_q_norm, mla_kv_norm, mla_w_uq, mla_w_ukv, mla_w_o,
           fox_w_in, fox_b_f, fox_q_norm, fox_k_norm, fox_w_o,
           rwkv_mu, rwkv_w_rkvg, rwkv_w0, rwkv_w1, rwkv_w2, rwkv_a0, rwkv_a1, rwkv_a2,
           rwkv_k_k, rwkv_k_a, rwkv_r_k, rwkv_lnx_w, rwkv_lnx_b, rwkv_w_o):
    n_b, seq, _ = x_prompt.shape
    n_db, n_new, _ = x_sample.shape
    n_pages = page_table.shape[1]
    mp, ms = n_b * seq, n_db * n_new
    assert seq % TQ == 0 and seq % TM == 0 and ms % TM == 0 and n_pages % PAGES_PER_STEP == 0
    assert n_new <= 8 and n_new * FOX_HEADS <= 128
    dims = (n_b, seq, n_db, n_new)
    depth = norm_pre.shape[0]
    x = jnp.concatenate([x_prompt.reshape(mp, D_MODEL), x_sample.reshape(ms, D_MODEL)], axis=0)
    tables = _rope_tables(n_b, seq, n_db, n_new, n_pages * PAGE)
    row = lambda a: a.reshape(1, -1)
    outs = {name: [] for name in ("ckv", "kpe", "k", "v", "lf", "wkv_p", "wkv_s", "sh_p", "sh_s")}
    for i in range(depth):
        j, kind = divmod(i, 3)
        gpre = row(norm_pre[i])
        if kind == 0:
            u, gate, ckv, kpe = _mla_layer(x, dims, page_table, cache_mla_ckv, cache_mla_kpe, j, tables, gpre,
                                           mla_w_in[j], row(mla_q_norm[j]), row(mla_kv_norm[j]),
                                           mla_w_uq[j], mla_w_ukv[j])
            w_o = mla_w_o[j]
            outs["ckv"].append(ckv)
            outs["kpe"].append(kpe)
        elif kind == 1:
            u, gate, k, v, lf = _fox_layer(x, dims, page_table, cache_fox_k, cache_fox_v, cache_fox_logf, j, gpre,
                                           fox_w_in[j], fox_b_f[j], fox_q_norm[j], fox_k_norm[j])
            w_o = fox_w_o[j]
            outs["k"].append(k)
            outs["v"].append(v)
            outs["lf"].append(lf)
        else:
            u, gate, s_p, s_s, sh_p, sh_s = _rwkv_layer(
                x, dims, state_rwkv_shift[j], state_rwkv_wkv[j], gpre, rwkv_mu[j], rwkv_w_rkvg[j],
                rwkv_w0[j], rwkv_w1[j], rwkv_w2[j], rwkv_a0[j], rwkv_a1[j], rwkv_a2[j],
                rwkv_k_k[j], rwkv_k_a[j], rwkv_r_k[j].reshape(-1), rwkv_lnx_w[j], rwkv_lnx_b[j])
            w_o = rwkv_w_o[j]
            outs["wkv_p"].append(s_p)
            outs["wkv_s"].append(s_s)
            outs["sh_p"].append(sh_p)
            outs["sh_s"].append(sh_s)
        x = _out_proj(*u, gate, x, w_o.astype(BF16), row(norm_post[i]))

    def split(name, *tail):
        a = jnp.stack(outs[name])
        return (a[:, :mp].reshape((a.shape[0], n_b, seq) + tail),
                a[:, mp:].reshape((a.shape[0], n_db, n_new) + tail))

    p_ckv, s_ckv = split("ckv", MLA_LORA)
    p_kpe, s_kpe = split("kpe", MLA_ROPE)
    p_k, s_k = split("k", FOX_KVH, FOX_HD)
    p_v, s_v = split("v", FOX_KVH, FOX_HD)
    p_lf, s_lf = split("lf", FOX_HEADS)
    return (x[:mp].reshape(n_b, seq, D_MODEL), x[mp:].reshape(n_db, n_new, D_MODEL),
            p_ckv, p_kpe, p_k, p_v, p_lf, jnp.stack(outs["wkv_p"]), jnp.stack(outs["sh_p"]),
            s_ckv, s_kpe, s_k, s_v, s_lf, jnp.stack(outs["wkv_s"]), jnp.stack(outs["sh_s"]))
```

```python
import functools
import math

import jax
import jax.numpy as jnp
from jax import lax
from jax.experimental import pallas as pl
from jax.experimental.pallas import tpu as pltpu

F32 = jnp.float32
BF16 = jnp.bfloat16

D_MODEL = 2048
NORM_EPS = 1e-6
PAGE = 128

MLA_NOPE = 128
MLA_ROPE = 64
MLA_V = 128
MLA_HEADS = 16
MLA_HEAD_GROUP = 4
MLA_LORA = 512
MLA_SCALE = (MLA_NOPE + MLA_ROPE) ** -0.5
MLA_KW = MLA_LORA + 128
ROPE_THETA = 10000.0

FOX_HD = 128
FOX_HEADS = 16
FOX_KVH = 4
FOX_GROUP = 4
FOX_SCALE = FOX_HD ** -0.5

RWKV_HD = 64
RWKV_HEADS = 32
RWKV_LORA_PAD = 128
RWKV_CHUNK = 64
RWKV_SCAN_HEADS = 8
RWKV_LNX_EPS = 64e-5

TM = 256
TQ = 256
PAGES_PER_STEP = 16
DEC_CHAINS = 4
NEG = -1e30
VMEM_LIMIT = 56 * 1024 * 1024
HI = lax.Precision.HIGHEST

NT = (((1,), (1,)), ((), ()))
TN = (((0,), (0,)), ((), ()))


def _params(sem):
    return pltpu.CompilerParams(dimension_semantics=sem, vmem_limit_bytes=VMEM_LIMIT)


def _resident(shape):
    nd = len(shape)
    return pl.BlockSpec(shape, lambda *_: (0,) * nd, pipeline_mode=pl.Buffered(1))


def _dot(a, b):
    return jnp.dot(a, b, preferred_element_type=F32)


def _dot_nt(a, b):
    return lax.dot_general(a, b, NT, preferred_element_type=F32)


def _rms(x, g):
    return x * lax.rsqrt(jnp.mean(x * x, axis=-1, keepdims=True) + NORM_EPS) * g


def _sigmoid(x):
    return 1.0 / (1.0 + jnp.exp(-x))


def _softplus(x):
    return jnp.maximum(x, 0.0) + jnp.log(1.0 + jnp.exp(-jnp.abs(x)))


def _mla_proj_kernel(x_ref, gpre_ref, w_ref, qn_ref, kvn_ref, wuqn_ref, wuqp_ref, wuqr_ref, wk_ref,
                     cos_ref, sin_ref, gate_ref, ckv_ref, kpe_ref, kcat_ref, qfull_ref):
    hb = _rms(x_ref[...], gpre_ref[...]).astype(BF16)
    cq = _rms(_dot(hb, w_ref[:, 0:512]), qn_ref[...]).astype(BF16)
    ckv = _rms(_dot(hb, w_ref[:, 512:1024]), kvn_ref[...])
    z = _dot(hb, w_ref[:, 1024:1152])
    cos_t = cos_ref[...]
    sin_t = sin_ref[...]
    lane = lax.broadcasted_iota(jnp.int32, z.shape, 1)
    kpe = jnp.where(lane < MLA_ROPE, z * cos_t + pltpu.roll(z, 64, 1) * sin_t, 0.0)
    gate_ref[...] = _dot(hb, w_ref[:, 1152:3200])
    ckv_ref[...] = ckv
    kpe_ref[...] = kpe[:, 0:MLA_ROPE]
    kcat_ref[:, 0:512] = ckv.astype(BF16)
    kcat_ref[:, 512:640] = kpe.astype(BF16)
    qn = _dot(cq, wuqn_ref[...]).astype(BF16)
    cos8 = jnp.concatenate([cos_t] * 8, axis=1)
    sin8 = jnp.concatenate([sin_t] * 8, axis=1)
    qpe = (_dot(cq, wuqp_ref[...]) * cos8 + _dot(cq, wuqr_ref[...]) * sin8) * MLA_SCALE
    for h in range(MLA_HEADS):
        ql = _dot(qn[:, h * 128:(h + 1) * 128], wk_ref[h]) * MLA_SCALE
        qfull_ref[h, :, 0:512] = ql.astype(BF16)
        blk = qpe[:, (h // 2) * 128:(h // 2 + 1) * 128]
        if h % 2:
            blk = pltpu.roll(blk, 64, 1)
        qfull_ref[h, :, 512:640] = jnp.where(lane < MLA_ROPE, blk, 0.0).astype(BF16)


def _mla_proj(x, gpre, w_main, qn, kvn, wuqn, wuqp, wuqr, wk, cos_t, sin_t):
    m = x.shape[0]
    row = lambda w: pl.BlockSpec((TM, w), lambda i: (i, 0))
    return pl.pallas_call(
        _mla_proj_kernel,
        grid=(m // TM,),
        in_specs=[row(D_MODEL), _resident((1, D_MODEL)), _resident(w_main.shape), _resident((1, 512)),
                  _resident((1, 512)), _resident(wuqn.shape), _resident(wuqp.shape), _resident(wuqr.shape),
                  _resident(wk.shape), row(128), row(128)],
        out_specs=[row(D_MODEL), row(512), row(MLA_ROPE), row(MLA_KW),
                   pl.BlockSpec((MLA_HEADS, TM, MLA_KW), lambda i: (0, i, 0))],
        out_shape=[jax.ShapeDtypeStruct((m, D_MODEL), F32), jax.ShapeDtypeStruct((m, 512), F32),
                   jax.ShapeDtypeStruct((m, MLA_ROPE), F32), jax.ShapeDtypeStruct((m, MLA_KW), BF16),
                   jax.ShapeDtypeStruct((MLA_HEADS, m, MLA_KW), BF16)],
        compiler_params=_params(("arbitrary",)),
        name="mla_proj",
    )(x, gpre, w_main, qn, kvn, wuqn, wuqp, wuqr, wk, cos_t, sin_t)


def _flash_update(scores, values, m_ref, l_ref, acc_ref, mxu_row_sum=False):
    probs, alphas = {}, {}
    for c, s in scores.items():
        m_prev = m_ref[c]
        m_new = jnp.maximum(m_prev, jnp.max(s, axis=-1, keepdims=True))
        alphas[c] = jnp.exp(m_prev - m_new)
        p = jnp.exp(s - m_new)
        if not mxu_row_sum:
            l_ref[c] = alphas[c] * l_ref[c] + jnp.sum(p, axis=-1, keepdims=True)
        m_ref[c] = m_new
        probs[c] = p.astype(BF16)
    for c in scores:
        acc_ref[c] = alphas[c] * acc_ref[c] + _dot(probs[c], values[c])
        if mxu_row_sum:
            ones = jnp.ones((probs[c].shape[1], 128), BF16)
            l_ref[c] = alphas[c] * l_ref[c] + _dot(probs[c], ones)[:, 0:1]


def _flash_init(m_ref, l_ref, acc_ref):
    m_ref[...] = jnp.full(m_ref.shape, NEG, F32)
    l_ref[...] = jnp.zeros(l_ref.shape, F32)
    acc_ref[...] = jnp.zeros(acc_ref.shape, F32)


def _flash_merge(m_ref, l_ref, acc_ref):
    n = m_ref.shape[0]
    m = functools.reduce(jnp.maximum, [m_ref[c] for c in range(n)])
    w = [jnp.exp(m_ref[c] - m) for c in range(n)]
    return sum(acc_ref[c] * w[c] for c in range(n)) / sum(l_ref[c] * w[c] for c in range(n))


def _flash_scratch(chains, rows, width):
    return [pltpu.VMEM((chains, rows, width), F32), pltpu.VMEM((chains, rows, 1), F32),
            pltpu.VMEM((chains, rows, 1), F32)]


def _mla_attn_prompt_kernel(q_ref, k_ref, wv_ref, o_ref, acc_ref, m_ref, l_ref):
    qi = pl.program_id(1)
    half = MLA_HEAD_GROUP // 2
    rows = half * TQ
    qs = [q_ref[c * half:(c + 1) * half].reshape(rows, MLA_KW) for c in range(2)]
    _flash_init(m_ref, l_ref, acc_ref)

    def keys(kb):
        return k_ref[pl.ds(pl.multiple_of(kb * TQ, TQ), TQ), :]

    def qk(kb):
        k = keys(kb)
        return tuple(_dot_nt(qs[c], k) for c in range(2))

    def update(kb, scores):
        v = keys(kb)[:, 0:MLA_LORA]
        _flash_update(dict(enumerate(scores)), {0: v, 1: v}, m_ref, l_ref, acc_ref)

    def body(kb, scores):
        nxt = qk(kb + 1)
        update(kb, scores)
        return nxt

    scores = lax.fori_loop(0, qi, body, qk(0))
    tok = lax.broadcasted_iota(jnp.int32, (rows, TQ), 0) & (TQ - 1)
    key = lax.broadcasted_iota(jnp.int32, (rows, TQ), 1)
    update(qi, tuple(jnp.where(key <= tok, s, NEG) for s in scores))
    for c in range(2):
        o = (acc_ref[c] / l_ref[c]).astype(BF16)
        for hh in range(half):
            h = c * half + hh
            o_ref[:, h * MLA_V:(h + 1) * MLA_V] = _dot(o[hh * TQ:(hh + 1) * TQ, :], wv_ref[h])


def _mla_attn_prompt(qfull, kcat, wv, n_batch, seq):
    nq = seq // TQ
    hg = MLA_HEAD_GROUP
    rows = hg // 2 * TQ
    return pl.pallas_call(
        _mla_attn_prompt_kernel,
        grid=(n_batch, nq, MLA_HEADS // hg),
        in_specs=[pl.BlockSpec((hg, TQ, MLA_KW), lambda b, q, g: (g, b * nq + q, 0)),
                  pl.BlockSpec((seq, MLA_KW), lambda b, q, g: (b, 0)),
                  pl.BlockSpec((hg, MLA_LORA, MLA_V), lambda b, q, g: (g, 0, 0))],
        out_specs=pl.BlockSpec((TQ, hg * MLA_V), lambda b, q, g: (b * nq + q, g)),
        out_shape=jax.ShapeDtypeStruct((n_batch * seq, D_MODEL), F32),
        scratch_shapes=_flash_scratch(2, rows, MLA_LORA),
        compiler_params=_params(("arbitrary", "arbitrary", "arbitrary")),
        name="mla_attn_prompt",
    )(qfull, kcat, wv)


def _page_copy(pt_ref, pool, buf, sem, layer, first, slot, i):
    return pltpu.make_async_copy(pool.at[layer, pt_ref[first + i]], buf.at[slot, i], sem.at[slot, i])


def _paged_step(pt_ref, pools, bufs, sems, layer, n_steps, consume):
    g_pages = PAGES_PER_STEP
    step = pl.program_id(0) * pl.num_programs(1) + pl.program_id(1)
    slot = step % 2
    copies = lambda s, sl: [_page_copy(pt_ref, pool, buf, sems.at[p], layer, s * g_pages, sl, i)
                            for i in range(g_pages) for p, (pool, buf) in enumerate(zip(pools, bufs))]

    @pl.when(step == 0)
    def _():
        for c in copies(step, slot):
            c.start()

    per_page = len(pools)
    mine = copies(step, slot)
    for i in range(g_pages):
        for c in mine[i * per_page:(i + 1) * per_page]:
            c.wait()
        consume(i, slot)
    nxt = jnp.minimum(step + 1, n_steps - 1)
    for c in copies(nxt, 1 - slot):
        c.start()

    def drain():
        for c in copies(nxt, 1 - slot):
            c.wait()
    return drain


def _mla_dec_kernel(pt_ref, q_ref, knew_ref, ckv_hbm, kpe_hbm, o_ref, cbuf, pbuf, sems,
                    kbuf, kpbuf, acc_ref, m_ref, l_ref, *, layer, n_pages, n_new):
    g_pages = PAGES_PER_STEP
    g = pl.program_id(1)
    n_groups = n_pages // g_pages
    part = g_pages * PAGE // DEC_CHAINS

    @pl.when(g == 0)
    def _():
        _flash_init(m_ref, l_ref, acc_ref)

    def consume(i, slot):
        kbuf[i * PAGE:(i + 1) * PAGE, :] = cbuf[slot, i].astype(BF16)
        kpbuf[:, i * PAGE:(i + 1) * PAGE] = pbuf[slot, i].astype(BF16)

    drain = _paged_step(pt_ref, (ckv_hbm, kpe_hbm), (cbuf, pbuf), sems, layer,
                        pl.num_programs(0) * n_groups, consume)
    q = q_ref[0]
    q_lat = q[:, 0:MLA_LORA]
    q_pe = q[:, MLA_LORA:MLA_LORA + MLA_ROPE]
    keys = {c: kbuf[c * part:(c + 1) * part, :] for c in range(DEC_CHAINS)}
    scores = {c: _dot_nt(q_lat, k) + _dot(q_pe, kpbuf[:, c * part:(c + 1) * part]) for c, k in keys.items()}
    _flash_update(scores, keys, m_ref, l_ref, acc_ref)

    @pl.when(g == n_pages // g_pages - 1)
    def _():
        kn = knew_ref[0]
        s = _dot_nt(q, kn)
        tok = lax.broadcasted_iota(jnp.int32, s.shape, 0) // MLA_HEADS
        key = lax.broadcasted_iota(jnp.int32, s.shape, 1)
        s = jnp.where((key <= tok) & (key < n_new), s, NEG)
        _flash_update({0: s}, {0: kn[:, 0:MLA_LORA]}, m_ref, l_ref, acc_ref)
        o = _flash_merge(m_ref, l_ref, acc_ref)
        o_ref[0] = o[0:n_new * MLA_HEADS, :].reshape(n_new, MLA_HEADS, MLA_LORA)

    @pl.when((pl.program_id(0) == pl.num_programs(0) - 1) & (g == n_groups - 1))
    def _():
        drain()


def _mla_attn_sample(page_table, q_s, knew, cache_ckv, cache_kpe_t, layer, n_new):
    n_b, n_pages = page_table.shape
    g_pages = PAGES_PER_STEP
    pt = page_table.reshape(-1)
    grid_spec = pltpu.PrefetchScalarGridSpec(
        num_scalar_prefetch=1,
        grid=(n_b, n_pages // g_pages),
        in_specs=[pl.BlockSpec((1, 128, MLA_KW), lambda b, g, pt_ref: (b, 0, 0)),
                  pl.BlockSpec((1, 16, MLA_KW), lambda b, g, pt_ref: (b, 0, 0)),
                  pl.BlockSpec(memory_space=pl.ANY), pl.BlockSpec(memory_space=pl.ANY)],
        out_specs=pl.BlockSpec((1, n_new, MLA_HEADS, MLA_LORA), lambda b, g, pt_ref: (b, 0, 0, 0)),
        scratch_shapes=[pltpu.VMEM((2, g_pages, PAGE, MLA_LORA), F32), pltpu.VMEM((2, g_pages, MLA_ROPE, PAGE), F32),
                        pltpu.SemaphoreType.DMA((2, 2, g_pages)),
                        pltpu.VMEM((g_pages * PAGE, MLA_LORA), BF16),
                        pltpu.VMEM((MLA_ROPE, g_pages * PAGE), BF16)] + _flash_scratch(DEC_CHAINS, 128, MLA_LORA),
    )
    return pl.pallas_call(
        functools.partial(_mla_dec_kernel, layer=layer, n_pages=n_pages, n_new=n_new),
        grid_spec=grid_spec,
        out_shape=jax.ShapeDtypeStruct((n_b, n_new, MLA_HEADS, MLA_LORA), F32),
        compiler_params=_params(("arbitrary", "arbitrary")),
        name="mla_attn_sample",
    )(pt, q_s, knew, cache_ckv, cache_kpe_t)


def _mla_up_kernel(o_ref, wv_ref, u_ref):
    for h in range(MLA_HEADS):
        oh = o_ref[:, h * MLA_LORA:(h + 1) * MLA_LORA].astype(BF16)
        u_ref[:, h * MLA_V:(h + 1) * MLA_V] = _dot(oh, wv_ref[h])


def _mla_up(o_lat, wv):
    m = o_lat.shape[0]
    return pl.pallas_call(
        _mla_up_kernel,
        grid=(m // TM,),
        in_specs=[pl.BlockSpec((TM, MLA_HEADS * MLA_LORA), lambda i: (i, 0)), _resident(wv.shape)],
        out_specs=pl.BlockSpec((TM, D_MODEL), lambda i: (i, 0)),
        out_shape=jax.ShapeDtypeStruct((m, D_MODEL), F32),
        compiler_params=_params(("arbitrary",)),
        name="mla_up",
    )(o_lat, wv)


def _head_rms(x, g, scale):
    return x * (lax.rsqrt(jnp.mean(x * x, axis=-1, keepdims=True) + NORM_EPS) * scale) * g


def _fox_proj_kernel(x_ref, gpre_ref, w_ref, bf_ref, qn_ref, kn_ref,
                     q_ref, k_ref, v_ref, kb_ref, vb_ref, lf_ref, c_ref, gate_ref, carry_ref, *, tiles_per_seq):
    i = pl.program_id(0)
    hb = _rms(x_ref[...], gpre_ref[...]).astype(BF16)
    q = _dot(hb, w_ref[:, 0:2048])
    for h in range(FOX_HEADS):
        sl = slice(h * FOX_HD, (h + 1) * FOX_HD)
        q_ref[:, sl] = _head_rms(q[:, sl], qn_ref[...], FOX_SCALE).astype(BF16)
    k = _dot(hb, w_ref[:, 2048:2560])
    for h in range(FOX_KVH):
        sl = slice(h * FOX_HD, (h + 1) * FOX_HD)
        kh = _head_rms(k[:, sl], kn_ref[...], 1.0)
        k_ref[:, sl] = kh
        kb_ref[:, sl] = kh.astype(BF16)
    v = _dot(hb, w_ref[:, 2560:3072])
    v_ref[...] = v
    vb_ref[...] = v.astype(BF16)
    fl = _dot(hb, w_ref[:, 3072:3200]) + bf_ref[...]
    lf = -_softplus(-fl)
    lf_ref[...] = lf[:, 0:FOX_HEADS]

    @pl.when(i % tiles_per_seq == 0)
    def _():
        carry_ref[...] = jnp.zeros(carry_ref.shape, F32)

    r = lax.broadcasted_iota(jnp.int32, (TM, TM), 0)
    c = lax.broadcasted_iota(jnp.int32, (TM, TM), 1)
    tri = (r >= c).astype(F32)
    cum = jnp.dot(tri, lf, preferred_element_type=F32, precision=HI) + carry_ref[...]
    c_ref[...] = cum[:, 0:FOX_HEADS]
    carry_ref[...] = cum[TM - 1:TM, :]
    gate_ref[...] = _dot(hb, w_ref[:, 3200:5248])


def _fox_proj(x, gpre, w_main, bf, qn, kn, tiles_per_seq):
    m = x.shape[0]
    row = lambda w: pl.BlockSpec((TM, w), lambda i: (i, 0))
    kvw = FOX_KVH * FOX_HD
    return pl.pallas_call(
        functools.partial(_fox_proj_kernel, tiles_per_seq=tiles_per_seq),
        grid=(m // TM,),
        in_specs=[row(D_MODEL), _resident((1, D_MODEL)), _resident(w_main.shape), _resident((1, 128)),
                  _resident((1, FOX_HD)), _resident((1, FOX_HD))],
        out_specs=[row(D_MODEL), row(kvw), row(kvw), row(kvw), row(kvw), row(FOX_HEADS), row(FOX_HEADS),
                   row(D_MODEL)],
        out_shape=[jax.ShapeDtypeStruct((m, D_MODEL), BF16), jax.ShapeDtypeStruct((m, kvw), F32),
                   jax.ShapeDtypeStruct((m, kvw), F32), jax.ShapeDtypeStruct((m, kvw), BF16),
                   jax.ShapeDtypeStruct((m, kvw), BF16), jax.ShapeDtypeStruct((m, FOX_HEADS), F32),
                   jax.ShapeDtypeStruct((m, FOX_HEADS), F32), jax.ShapeDtypeStruct((m, D_MODEL), F32)],
        scratch_shapes=[pltpu.VMEM((1, 128), F32)],
        compiler_params=_params(("arbitrary",)),
        name="fox_proj",
    )(x, gpre, w_main, bf, qn, kn)


def _fox_attn_prompt_kernel(q_ref, k_ref, v_ref, c_ref, o_ref, acc_ref, m_ref, l_ref):
    qi = pl.program_id(2)
    half = FOX_GROUP // 2
    rows = half * TQ
    qs = [jnp.concatenate([q_ref[:, g * FOX_HD:(g + 1) * FOX_HD] for g in range(c * half, (c + 1) * half)], axis=0)
          for c in range(2)]
    _flash_init(m_ref, l_ref, acc_ref)

    def qk(kb):
        k = k_ref[pl.ds(pl.multiple_of(kb * TQ, TQ), TQ), :]
        return tuple(_dot_nt(qs[c], k) for c in range(2))

    def update(kb, raw, masked):
        start = pl.multiple_of(kb * TQ, TQ)
        v = v_ref[pl.ds(start, TQ), :]
        cb = c_ref[0, 0, :, pl.ds(start, TQ)]
        scores = {c: (s.reshape(half, TQ, TQ) - cb[c * half:(c + 1) * half, None, :]).reshape(rows, TQ)
                  for c, s in enumerate(raw)}
        if masked:
            tok = lax.broadcasted_iota(jnp.int32, (rows, TQ), 0) & (TQ - 1)
            key = lax.broadcasted_iota(jnp.int32, (rows, TQ), 1)
            scores = {c: jnp.where(key <= tok, s, NEG) for c, s in scores.items()}
        _flash_update(scores, {0: v, 1: v}, m_ref, l_ref, acc_ref, mxu_row_sum=True)

    def body(kb, raw):
        nxt = qk(kb + 1)
        update(kb, raw, False)
        return nxt

    update(qi, lax.fori_loop(0, qi, body, qk(0)), True)
    for c in range(2):
        o = acc_ref[c] / l_ref[c]
        for gg in range(half):
            g = c * half + gg
            o_ref[:, g * FOX_HD:(g + 1) * FOX_HD] = o[gg * TQ:(gg + 1) * TQ, :]


def _fox_attn_prompt(q, kb, vb, c_t, n_batch, seq):
    nq = seq // TQ
    rows = FOX_GROUP // 2 * TQ
    gw = FOX_GROUP * FOX_HD
    return pl.pallas_call(
        _fox_attn_prompt_kernel,
        grid=(n_batch, FOX_KVH, nq),
        in_specs=[pl.BlockSpec((TQ, gw), lambda b, h, qi: (b * nq + qi, h)),
                  pl.BlockSpec((seq, FOX_HD), lambda b, h, qi: (b, h)),
                  pl.BlockSpec((seq, FOX_HD), lambda b, h, qi: (b, h)),
                  pl.BlockSpec((1, 1, FOX_GROUP, seq), lambda b, h, qi: (b, h, 0, 0))],
        out_specs=pl.BlockSpec((TQ, gw), lambda b, h, qi: (b * nq + qi, h)),
        out_shape=jax.ShapeDtypeStruct((n_batch * seq, D_MODEL), F32),
        scratch_shapes=_flash_scratch(2, rows, FOX_HD),
        compiler_params=_params(("arbitrary", "arbitrary", "arbitrary")),
        name="fox_attn_prompt",
    )(q, kb, vb, c_t)


def _lane_cumsum(x):
    lane = lax.broadcasted_iota(jnp.int32, x.shape, 1)
    d = 1
    while d < x.shape[1]:
        x = x + jnp.where(lane >= d, pltpu.roll(x, d, 1), 0.0)
        d *= 2
    return x


def _fox_dec_kernel(pt_ref, q_ref, knew_ref, vnew_ref, lfnew_ref, k_hbm, v_hbm, lf_hbm, o_ref,
                    kraw, vraw, lraw, sems, kbuf, vbuf, lfbuf, carry_ref, acc_ref, m_ref, l_ref,
                    *, layer, n_pages, n_new):
    g_pages = PAGES_PER_STEP
    g = pl.program_id(1)
    n_groups = n_pages // g_pages
    kvw = FOX_KVH * FOX_HD
    part = g_pages * PAGE // DEC_CHAINS

    @pl.when(g == 0)
    def _():
        _flash_init(m_ref, l_ref, acc_ref)
        carry_ref[...] = jnp.zeros(carry_ref.shape, F32)

    def consume(i, slot):
        for h in range(FOX_KVH):
            sl = slice(h * FOX_HD, (h + 1) * FOX_HD)
            kbuf[i * PAGE:(i + 1) * PAGE, sl] = kraw[slot, i, pl.ds(h, PAGE, stride=FOX_KVH), :].astype(BF16)
            vbuf[i * PAGE:(i + 1) * PAGE, sl] = vraw[slot, i, pl.ds(h, PAGE, stride=FOX_KVH), :].astype(BF16)
        lfbuf[i * FOX_HEADS:(i + 1) * FOX_HEADS, :] = lraw[slot, i]

    drain = _paged_step(pt_ref, (k_hbm, v_hbm, lf_hbm), (kraw, vraw, lraw), sems, layer,
                        pl.num_programs(0) * n_groups, consume)

    lf_t = lfbuf[...]
    cum = _lane_cumsum(lf_t)
    tot = jnp.broadcast_to(cum[:, PAGE - 1:PAGE], cum.shape)
    inc = tot
    d = FOX_HEADS
    while d < g_pages * FOX_HEADS:
        inc = inc + jnp.concatenate([jnp.zeros((d, PAGE), F32), inc[:-d, :]], axis=0)
        d *= 2
    carry = carry_ref[...]
    c_keys = cum + (inc - tot) + jnp.concatenate([carry] * g_pages, axis=0)
    carry_ref[...] = carry + inc[(g_pages - 1) * FOX_HEADS:, :]

    q = q_ref[0]
    row_kvh = (lax.broadcasted_iota(jnp.int32, (128, kvw), 0) % FOX_HEADS) // FOX_GROUP
    col_kvh = lax.broadcasted_iota(jnp.int32, (128, kvw), 1) // FOX_HD
    own = row_kvh == col_kvh
    qbd = jnp.where(own, jnp.concatenate([q] * FOX_KVH, axis=1), jnp.zeros((), BF16))
    bias = jnp.concatenate(
        [jnp.concatenate([c_keys[i * FOX_HEADS:(i + 1) * FOX_HEADS, :]] * (128 // FOX_HEADS), axis=0)
         for i in range(g_pages)], axis=1)
    halves = {c: slice(c * part, (c + 1) * part) for c in range(DEC_CHAINS)}
    scores = {c: _dot_nt(qbd, kbuf[ks, :]) for c, ks in halves.items()}
    scores = {c: scores[c] - bias[:, ks] for c, ks in halves.items()}
    _flash_update(scores, {c: vbuf[ks, :] for c, ks in halves.items()}, m_ref, l_ref, acc_ref)

    @pl.when(g == n_pages // g_pages - 1)
    def _():
        c_new = carry_ref[...] + _lane_cumsum(lfnew_ref[0])
        bias_n = jnp.concatenate([c_new] * (128 // FOX_HEADS), axis=0)[:, 0:16]
        s = _dot_nt(qbd, knew_ref[0]) - bias_n
        tok = lax.broadcasted_iota(jnp.int32, s.shape, 0) // FOX_HEADS
        key = lax.broadcasted_iota(jnp.int32, s.shape, 1)
        s = jnp.where((key <= tok) & (key < n_new), s, NEG)
        _flash_update({0: s}, {0: vnew_ref[0]}, m_ref, l_ref, acc_ref)
        o = jnp.where(own, _flash_merge(m_ref, l_ref, acc_ref), 0.0)
        o = o[:, 0:128] + o[:, 128:256] + o[:, 256:384] + o[:, 384:512]
        o_ref[0] = o[0:n_new * FOX_HEADS, :].reshape(n_new, FOX_HEADS, FOX_HD)

    @pl.when((pl.program_id(0) == pl.num_programs(0) - 1) & (g == n_groups - 1))
    def _():
        drain()


def _fox_attn_sample(page_table, q_s, knew, vnew, lfnew_t, cache_k, cache_v, cache_lf_t, layer, n_new):
    n_b, n_pages = page_table.shape
    g_pages = PAGES_PER_STEP
    pt = page_table.reshape(-1)
    kvw = FOX_KVH * FOX_HD
    per_b = lambda shape: pl.BlockSpec((1,) + shape, lambda b, g, pt_ref: (b, 0, 0))
    hbm = pl.BlockSpec(memory_space=pl.ANY)
    grid_spec = pltpu.PrefetchScalarGridSpec(
        num_scalar_prefetch=1,
        grid=(n_b, n_pages // g_pages),
        in_specs=[per_b((128, FOX_HD)), per_b((16, kvw)), per_b((16, kvw)), per_b((FOX_HEADS, 128)), hbm, hbm, hbm],
        out_specs=pl.BlockSpec((1, n_new, FOX_HEADS, FOX_HD), lambda b, g, pt_ref: (b, 0, 0, 0)),
        scratch_shapes=[pltpu.VMEM((2, g_pages, PAGE * FOX_KVH, FOX_HD), F32),
                        pltpu.VMEM((2, g_pages, PAGE * FOX_KVH, FOX_HD), F32),
                        pltpu.VMEM((2, g_pages, FOX_HEADS, PAGE), F32),
                        pltpu.SemaphoreType.DMA((3, 2, g_pages)),
                        pltpu.VMEM((g_pages * PAGE, kvw), BF16), pltpu.VMEM((g_pages * PAGE, kvw), BF16),
                        pltpu.VMEM((g_pages * FOX_HEADS, PAGE), F32),
                        pltpu.VMEM((FOX_HEADS, 128), F32)] + _flash_scratch(DEC_CHAINS, 128, kvw),
    )
    return pl.pallas_call(
        functools.partial(_fox_dec_kernel, layer=layer, n_pages=n_pages, n_new=n_new),
        grid_spec=grid_spec,
        out_shape=jax.ShapeDtypeStruct((n_b, n_new, FOX_HEADS, FOX_HD), F32),
        compiler_params=_params(("arbitrary", "arbitrary")),
        name="fox_attn_sample",
    )(pt, q_s, knew, vnew, lfnew_t, cache_k, cache_v, cache_lf_t)


def _rms_kernel(x_ref, g_ref, h_ref):
    h_ref[...] = _rms(x_ref[...], g_ref[...])


def _rms_norm(x, g):
    m = x.shape[0]
    row = pl.BlockSpec((TM, D_MODEL), lambda i: (i, 0))
    return pl.pallas_call(
        _rms_kernel, grid=(m // TM,), in_specs=[row, _resident((1, D_MODEL))], out_specs=row,
        out_shape=jax.ShapeDtypeStruct((m, D_MODEL), F32), compiler_params=_params(("arbitrary",)),
        name="rms_norm",
    )(x, g)


def _rwkv_mix_kernel(h_ref, hp_ref, mu_ref, w_ref, o_ref):
    h = h_ref[...]
    xm = h + (hp_ref[...] - h) * mu_ref[0]
    o_ref[0] = _dot(xm.astype(BF16), w_ref[0])


def _rwkv_mix_proj(h, hp, mu4, w4):
    m = h.shape[0]
    row = pl.BlockSpec((TM, D_MODEL), lambda p, i: (i, 0))
    return pl.pallas_call(
        _rwkv_mix_kernel,
        grid=(4, m // TM),
        in_specs=[row, row, pl.BlockSpec((1, 1, D_MODEL), lambda p, i: (p, 0, 0)),
                  pl.BlockSpec((1, D_MODEL, D_MODEL), lambda p, i: (p, 0, 0))],
        out_specs=pl.BlockSpec((1, TM, D_MODEL), lambda p, i: (p, i, 0)),
        out_shape=jax.ShapeDtypeStruct((4, m, D_MODEL), F32),
        compiler_params=_params(("arbitrary", "arbitrary")),
        name="rwkv_mix_proj",
    )(h, hp, mu4, w4)


def _split2(x):
    hi = x.astype(BF16)
    return hi, (x - hi.astype(F32)).astype(BF16)


def _head_sums(x):
    r = lax.broadcasted_iota(jnp.int32, (256, 128), 0)
    c = lax.broadcasted_iota(jnp.int32, (256, 128), 1)
    ones2 = ((r & 127) // RWKV_HD == c // RWKV_HD).astype(BF16)
    out = []
    for j in range(x.shape[1] // 128):
        hi, lo = _split2(x[:, j * 128:(j + 1) * 128])
        out.append(_dot(jnp.concatenate([hi, lo], axis=1), ones2))
    return jnp.concatenate(out, axis=1)


def _rwkv_post_kernel(h_ref, hp_ref, r_ref, k_ref, v_ref, muw_ref, mua_ref, w0_ref, w1_ref, w2_ref,
                      a0_ref, a1_ref, a2_ref, kk_ref, ka_ref, rk_ref, *outs, chunked):
    h = h_ref[...]
    dh = hp_ref[...] - h
    xw = (h + dh * muw_ref[...]).astype(BF16)
    xa = (h + dh * mua_ref[...]).astype(BF16)
    wl = w0_ref[...] + _dot(jnp.tanh(_dot(xw, w1_ref[...])).astype(BF16), w2_ref[...])
    lw = -jnp.exp(-_softplus(-wl) - 0.5)
    asig = _sigmoid(a0_ref[...] + _dot(_dot(xa, a1_ref[...]).astype(BF16), a2_ref[...]))
    k = k_ref[0]
    k4 = k * (1.0 + (asig - 1.0) * ka_ref[...])
    kr = k * kk_ref[...]
    kk = kr * lax.rsqrt(jnp.maximum(_head_sums(kr * kr), 1e-24))
    a_vec = -kk
    b_vec = kk * asig
    if not chunked:
        w_ref, k4_ref, a_ref, b_ref = outs
        w_ref[...] = jnp.exp(lw)
        k4_ref[...] = k4
        a_ref[...] = a_vec
        b_ref[...] = b_vec
        return
    x1_ref, x2_ref, bonus_ref, pin_ref = outs
    r = r_ref[0]
    ri = lax.broadcasted_iota(jnp.int32, (TM, TM), 0)
    ci = lax.broadcasted_iota(jnp.int32, (TM, TM), 1)
    tri = ((ri >= ci) & (ri // RWKV_CHUNK == ci // RWKV_CHUNK)).astype(BF16)
    hi = lw.astype(BF16)
    mid, lo = _split2(lw - hi.astype(F32))
    cl =_dot(jnp.concatenate([tri, tri, tri], axis=1), jnp.concatenate([hi, mid, lo], axis=0))
    p_in = jnp.exp(cl)
    p_inv = jnp.exp(-cl)
    at = (a_vec * jnp.exp(cl - lw)).astype(BF16)
    rt = (r * p_in).astype(BF16)
    bt = (b_vec * p_inv).astype(BF16)
    kt = (k4 * p_inv).astype(BF16)
    n = RWKV_CHUNK
    for c in range(TM // n):
        x1_ref[2 * n * c:2 * n * c + n, :] = at[n * c:n * (c + 1), :]
        x1_ref[2 * n * c + n:2 * n * (c + 1), :] = rt[n * c:n * (c + 1), :]
        x2_ref[2 * n * c:2 * n * c + n, :] = bt[n * c:n * (c + 1), :]
        x2_ref[2 * n * c + n:2 * n * (c + 1), :] = kt[n * c:n * (c + 1), :]
    bonus_ref[...] = _head_sums(r * k4 * rk_ref[...]) * v_ref[0]
    pin_ref[...] = p_in


def _rwkv_post(h, hp, rkvg, vecs, w1, w2, a1, a2, row0, n_rows, chunked):
    t0 = row0 // TM
    row = pl.BlockSpec((TM, D_MODEL), lambda i: (i + t0, 0))
    proj = lambda p: pl.BlockSpec((1, TM, D_MODEL), lambda i: (p, i + t0, 0))
    out_row = pl.BlockSpec((TM, D_MODEL), lambda i: (i, 0))
    vec = _resident((1, D_MODEL))
    muw, mua, w0, a0, kk, ka, rk = vecs
    if chunked:
        stacked = pl.BlockSpec((2 * TM, D_MODEL), lambda i: (i, 0))
        out_specs = [stacked, stacked, out_row, out_row]
        out_shape = [jax.ShapeDtypeStruct((2 * n_rows, D_MODEL), BF16)] * 2 + \
                    [jax.ShapeDtypeStruct((n_rows, D_MODEL), F32)] * 2
    else:
        out_specs = [out_row] * 4
        out_shape = [jax.ShapeDtypeStruct((n_rows, D_MODEL), F32)] * 4
    return pl.pallas_call(
        functools.partial(_rwkv_post_kernel, chunked=chunked),
        grid=(n_rows // TM,),
        in_specs=[row, row, proj(0), proj(1), proj(2), vec, vec, vec, _resident(w1.shape), _resident(w2.shape),
                  vec, _resident(a1.shape), _resident(a2.shape), vec, vec, vec],
        out_specs=out_specs,
        out_shape=out_shape,
        compiler_params=_params(("arbitrary",)),
        name="rwkv_post_chunked" if chunked else "rwkv_post_steps",
    )(h, hp, rkvg, rkvg, rkvg, muw, mua, w0, w1, w2, a0, a1, a2, kk, ka, rk)


def _rwkv_chunks(states, x1s, x2s, vs):
    heads = range(len(vs))
    n = vs[0].shape[0]
    row = lax.broadcasted_iota(jnp.int32, (2 * n, 2 * n), 0)
    col = lax.broadcasted_iota(jnp.int32, (2 * n, 2 * n), 1) & (n - 1)
    keep = (row > col) & ((row < n) | (row - n >= col))
    zeros = jnp.zeros((n, RWKV_HD), F32)
    gs = [jnp.where(keep, _dot_nt(x1s[h], x2s[h]), 0.0).astype(BF16) for h in heads]
    zs = [_dot_nt(x1s[h], states[h].astype(BF16)) for h in heads]
    us = [zs[h][0:n] + _dot(gs[h][0:n], jnp.concatenate([zeros, vs[h]], axis=0).astype(BF16)) for h in heads]
    nks = [gs[h][0:n, 0:n] for h in heads]
    levels = int(math.log2(n))
    for lvl in range(levels):
        us = [us[h] + _dot(nks[h], us[h].astype(BF16)) for h in heads]
        if lvl + 1 < levels:
            nks = [_dot(nks[h], nks[h]).astype(BF16) for h in heads]
    uvs = [jnp.concatenate([us[h], vs[h]], axis=0).astype(BF16) for h in heads]
    ys = [zs[h][n:2 * n] + _dot(gs[h][n:2 * n], uvs[h]) for h in heads]
    new_states = [states[h] + lax.dot_general(uvs[h], x2s[h], TN, preferred_element_type=F32) for h in heads]
    return new_states, ys


def _rwkv_scan_kernel(x1_ref, x2_ref, v_ref, bonus_ref, pin_ref, lnw_ref, lnb_ref,
                      u_ref, sout_ref, state_ref, *, heads, n_chunks):
    c = pl.program_id(2)

    @pl.when(c == 0)
    def _():
        state_ref[...] = jnp.zeros(state_ref.shape, F32)

    sls = [slice(h * RWKV_HD, (h + 1) * RWKV_HD) for h in range(heads)]
    new_states, ys = _rwkv_chunks([state_ref[h] for h in range(heads)], [x1_ref[:, sl] for sl in sls],
                                  [x2_ref[:, sl] for sl in sls], [v_ref[0, :, sl] for sl in sls])
    outs = []
    for h, sl in enumerate(sls):
        state_ref[h] = new_states[h] * pin_ref[7:8, sl]
        y = ys[h]
        mean = jnp.mean(y, axis=-1, keepdims=True)
        var = jnp.mean(jnp.square(y - mean), axis=-1, keepdims=True)
        outs.append((y - mean) * lax.rsqrt(var + RWKV_LNX_EPS) * lnw_ref[:, sl] + lnb_ref[:, sl] + bonus_ref[:, sl])
    u_ref[...] = jnp.concatenate(outs, axis=-1)

    @pl.when(c == n_chunks - 1)
    def _():
        sout_ref[0] = state_ref[...]


def _rwkv_scan(x1, x2, rkvg, bonus, pin, lnw, lnb, n_seq, seq):
    n = RWKV_CHUNK
    heads = RWKV_SCAN_HEADS
    n_chunks = seq // n
    w = heads * RWKV_HD
    tok = lambda rows, scale: pl.BlockSpec((rows, w), lambda s, hg, c: (scale * (s * n_chunks + c), hg))
    last8 = pl.BlockSpec((8, w), lambda s, hg, c: ((s * n_chunks + c + 1) * (n // 8) - 1, hg))
    vec = pl.BlockSpec((1, w), lambda s, hg, c: (0, hg))
    return pl.pallas_call(
        functools.partial(_rwkv_scan_kernel, heads=heads, n_chunks=n_chunks),
        grid=(n_seq, RWKV_HEADS // heads, n_chunks),
        in_specs=[tok(2 * n, 1), tok(2 * n, 1),
                  pl.BlockSpec((1, n, w), lambda s, hg, c: (2, s * n_chunks + c, hg)),
                  tok(n, 1), last8, vec, vec],
        out_specs=[tok(n, 1), pl.BlockSpec((1, heads, RWKV_HD, RWKV_HD), lambda s, hg, c: (s, hg, 0, 0))],
        out_shape=[jax.ShapeDtypeStruct((n_seq * seq, D_MODEL), F32),
                   jax.ShapeDtypeStruct((n_seq, RWKV_HEADS, RWKV_HD, RWKV_HD), F32)],
        scratch_shapes=[pltpu.VMEM((heads, RWKV_HD, RWKV_HD), F32)],
        compiler_params=_params(("arbitrary", "arbitrary", "arbitrary")),
        name="rwkv_scan",
    )(x1, x2, rkvg, bonus, pin, lnw, lnb)


def _rwkv_steps_kernel(r_ref, w_ref, k_ref, v_ref, a_ref, b_ref, s0_ref, rk_ref, lnw_ref, lnb_ref,
                       u_ref, sout_ref, y_ref, *, heads, n_new):
    for h in range(heads):
        hs = slice(h * RWKV_HD, (h + 1) * RWKV_HD)

        def body(vi, carry):
            s = s0_ref[h, vi]
            for t in range(n_new):
                sa = jnp.sum(s * a_ref[t, hs, :], axis=0, keepdims=True)
                vv = v_ref[t, pl.ds(h * RWKV_HD + vi, 1), :]
                s = s * w_ref[t, hs, :] + sa * b_ref[t, hs, :] + vv * k_ref[t, hs, :]
                y_ref[t, pl.ds(h * RWKV_HD + vi, 1), :] = jnp.sum(s * r_ref[t, hs, :], axis=0, keepdims=True)
            sout_ref[h, vi] = s
            return carry

        lax.fori_loop(0, RWKV_HD, body, 0)
        for t in range(n_new):
            y = y_ref[t, hs, :]
            mean = jnp.mean(y, axis=0, keepdims=True)
            var = jnp.mean(jnp.square(y - mean), axis=0, keepdims=True)
            yn = (y - mean) * lax.rsqrt(var + RWKV_LNX_EPS) * lnw_ref[hs, :] + lnb_ref[hs, :]
            bonus = jnp.sum(r_ref[t, hs, :] * k_ref[t, hs, :] * rk_ref[hs, :], axis=0, keepdims=True)
            u_ref[t, hs, :] = yn + bonus * v_ref[t, hs, :]


def _rwkv_steps(r, w, k, v, a, b, s0_t, rk, lnw, lnb):
    n_new, _, n_db = r.shape
    heads = 2
    rows = heads * RWKV_HD
    tok = pl.BlockSpec((n_new, rows, n_db), lambda g: (0, g, 0))
    vec = pl.BlockSpec((rows, n_db), lambda g: (g, 0))
    st = pl.BlockSpec((heads, RWKV_HD, RWKV_HD, n_db), lambda g: (g, 0, 0, 0))
    return pl.pallas_call(
        functools.partial(_rwkv_steps_kernel, heads=heads, n_new=n_new),
        grid=(RWKV_HEADS // heads,),
        in_specs=[tok] * 6 + [st, vec, vec, vec],
        out_specs=[tok, st],
        out_shape=[jax.ShapeDtypeStruct((n_new, D_MODEL, n_db), F32),
                   jax.ShapeDtypeStruct((RWKV_HEADS, RWKV_HD, RWKV_HD, n_db), F32)],
        scratch_shapes=[pltpu.VMEM((n_new, rows, n_db), F32)],
        compiler_params=_params(("arbitrary",)),
        name="rwkv_steps",
    )(r, w, k, v, a, b, s0_t, rk, lnw, lnb)


def _out_proj_kernel(up_ref, us_ref, g_ref, x_ref, w_ref, gp_ref, o_ref, *, prompt_tiles):
    g = g_ref[...]
    u = jnp.where(pl.program_id(0) < prompt_tiles, up_ref[...], us_ref[...])
    a = (u * (g * _sigmoid(g))).astype(BF16)
    o_ref[...] = x_ref[...] + _rms(_dot(a, w_ref[...]), gp_ref[...])


def _out_proj(u_prompt, u_sample, gate, x, w_o, gpost):
    m = x.shape[0]
    pt = u_prompt.shape[0] // TM
    row = pl.BlockSpec((TM, D_MODEL), lambda i: (i, 0))
    if gate.ndim == 3:
        gate_spec = pl.BlockSpec((None, TM, D_MODEL), lambda i: (gate.shape[0] - 1, i, 0))
    else:
        gate_spec = row
    return pl.pallas_call(
        functools.partial(_out_proj_kernel, prompt_tiles=pt), grid=(m // TM,),
        in_specs=[pl.BlockSpec((TM, D_MODEL), lambda i: (jnp.minimum(i, pt - 1), 0)),
                  pl.BlockSpec((TM, D_MODEL), lambda i: (jnp.maximum(i - pt, 0), 0)),
                  gate_spec, row, _resident(w_o.shape), _resident((1, D_MODEL))], out_specs=row,
        out_shape=jax.ShapeDtypeStruct((m, D_MODEL), F32), compiler_params=_params(("arbitrary",)),
        name="out_proj",
    )(u_prompt, u_sample, gate, x, w_o, gpost)


def _mla_layer(x, dims, page_table, cache_ckv, cache_kpe, layer, tables, gpre, w_in, q_norm, kv_norm, w_uq, w_ukv):
    n_b, seq, n_db, n_new = dims
    mp = n_b * seq
    w_main = jnp.concatenate(
        [w_in[:, 0:1024], w_in[:, 1024:1088],
         -w_in[:, 1056:1088], w_in[:, 1024:1056], w_in[:, 1088:]], axis=1).astype(BF16)
    uq3 = w_uq.reshape(MLA_LORA, MLA_HEADS, MLA_NOPE + MLA_ROPE)
    wuqn = uq3[:, :, 0:MLA_NOPE].reshape(MLA_LORA, -1).astype(BF16)
    pe = uq3[:, :, MLA_NOPE:]
    wuqp = pe.reshape(MLA_LORA, -1).astype(BF16)
    wuqr = jnp.concatenate([-pe[:, :, 32:], pe[:, :, :32]], axis=-1).reshape(MLA_LORA, -1).astype(BF16)
    wk = jnp.transpose(w_ukv[:, :, 0:MLA_NOPE], (1, 2, 0)).astype(BF16)
    wv = jnp.transpose(w_ukv[:, :, MLA_NOPE:], (1, 0, 2)).astype(BF16)
    gate, ckv, kpe, kcat, qfull = _mla_proj(x, gpre, w_main, q_norm, kv_norm, wuqn, wuqp, wuqr, wk, *tables)
    u_p = _mla_attn_prompt(qfull, kcat, wv, n_b, seq)
    q_s = jnp.transpose(qfull[:, mp:, :], (1, 0, 2)).reshape(n_db, n_new * MLA_HEADS, MLA_KW)
    q_s = jnp.pad(q_s, ((0, 0), (0, 128 - n_new * MLA_HEADS), (0, 0)))
    knew = jnp.pad(kcat[mp:].reshape(n_db, n_new, MLA_KW), ((0, 0), (0, 16 - n_new), (0, 0)))
    o_lat = _mla_attn_sample(page_table, q_s, knew, cache_ckv, jnp.swapaxes(cache_kpe, 2, 3), layer, n_new)
    u_s = _mla_up(o_lat.reshape(n_db * n_new, MLA_HEADS * MLA_LORA), wv)
    return (u_p, u_s), gate, ckv, kpe


def _fox_layer(x, dims, page_table, cache_k, cache_v, cache_lf, layer, gpre, w_in, b_f, q_norm, k_norm):
    n_b, seq, n_db, n_new = dims
    mp = n_b * seq
    kvw = FOX_KVH * FOX_HD
    nq = FOX_HEADS * FOX_HD
    w_main = jnp.concatenate(
        [w_in[:, 0:nq + 2 * kvw], w_in[:, nq + 2 * kvw:nq + 2 * kvw + FOX_HEADS],
         jnp.zeros((D_MODEL, 128 - FOX_HEADS), F32), w_in[:, nq + 2 * kvw + FOX_HEADS:]], axis=1).astype(BF16)
    bf = jnp.pad(b_f, (0, 128 - FOX_HEADS)).reshape(1, 128)
    q, k, v, kb, vb, lf, c, gate = _fox_proj(x, gpre, w_main, bf, q_norm.reshape(1, -1), k_norm.reshape(1, -1),
                                             seq // TM)
    c_t = jnp.transpose(c[:mp].reshape(n_b, seq, FOX_HEADS), (0, 2, 1)).reshape(n_b, FOX_KVH, FOX_GROUP, seq)
    u_p = _fox_attn_prompt(q, kb, vb, c_t, n_b, seq)
    q_s = jnp.pad(q[mp:].reshape(n_db, n_new * FOX_HEADS, FOX_HD), ((0, 0), (0, 128 - n_new * FOX_HEADS), (0, 0)))
    knew = jnp.pad(kb[mp:].reshape(n_db, n_new, kvw), ((0, 0), (0, 16 - n_new), (0, 0)))
    vnew = jnp.pad(vb[mp:].reshape(n_db, n_new, kvw), ((0, 0), (0, 16 - n_new), (0, 0)))
    lfnew_t = jnp.pad(jnp.transpose(lf[mp:].reshape(n_db, n_new, FOX_HEADS), (0, 2, 1)),
                      ((0, 0), (0, 0), (0, 128 - n_new)))
    n_pool = cache_k.shape[1]
    o_s = _fox_attn_sample(page_table, q_s, knew, vnew, lfnew_t,
                           cache_k.reshape(cache_k.shape[0], n_pool, PAGE * FOX_KVH, FOX_HD),
                           cache_v.reshape(cache_v.shape[0], n_pool, PAGE * FOX_KVH, FOX_HD),
                           jnp.swapaxes(cache_lf, 2, 3), layer, n_new)
    return (u_p, o_s.reshape(n_db * n_new, D_MODEL)), gate, k, v, lf


def _rwkv_layer(x, dims, shift0, s0, gpre, mu, w_rkvg, w0, w1, w2, a0, a1, a2, k_k, k_a, r_k, lnx_w, lnx_b):
    n_b, seq, n_db, n_new = dims
    mp = n_b * seq
    vec = lambda a: a.reshape(1, D_MODEL)
    h = _rms_norm(x, gpre)
    h_p = h[:mp].reshape(n_b, seq, D_MODEL)
    h_s = h[mp:].reshape(n_db, n_new, D_MODEL)
    hp = jnp.concatenate(
        [jnp.concatenate([jnp.zeros((n_b, 1, D_MODEL), F32), h_p[:, :-1]], axis=1).reshape(mp, D_MODEL),
         jnp.concatenate([shift0[:, None, :], h_s[:, :-1]], axis=1).reshape(n_db * n_new, D_MODEL)], axis=0)
    rkvg = _rwkv_mix_proj(h, hp, mu[0:4].reshape(4, 1, D_MODEL), w_rkvg.astype(BF16))
    lora = w1.shape[1]
    w1p = jnp.pad(w1, ((0, 0), (0, RWKV_LORA_PAD - lora))).astype(BF16)
    w2p = jnp.pad(w2, ((0, RWKV_LORA_PAD - lora), (0, 0))).astype(BF16)
    a1p = jnp.pad(a1, ((0, 0), (0, RWKV_LORA_PAD - lora))).astype(BF16)
    a2p = jnp.pad(a2, ((0, RWKV_LORA_PAD - lora), (0, 0))).astype(BF16)
    vecs = (vec(mu[4]), vec(mu[5]), vec(w0), vec(a0), vec(k_k), vec(k_a), vec(r_k))
    x1, x2, bonus, pin = _rwkv_post(h, hp, rkvg, vecs, w1p, w2p, a1p, a2p, 0, mp, True)
    u_p, s_p = _rwkv_scan(x1, x2, rkvg, bonus, pin, vec(lnx_w), vec(lnx_b), n_b, seq)
    w_s, k_s, a_s, b_s = _rwkv_post(h, hp, rkvg, vecs, w1p, w2p, a1p, a2p, mp, n_db * n_new, False)
    to_lanes = lambda a: jnp.transpose(a.reshape(n_db, n_new, D_MODEL), (1, 2, 0))
    col = lambda a: jnp.broadcast_to(a.reshape(D_MODEL, 1), (D_MODEL, n_db))
    u_t, s_t = _rwkv_steps(to_lanes(rkvg[0, mp:]), to_lanes(w_s), to_lanes(k_s), to_lanes(rkvg[2, mp:]),
                           to_lanes(a_s), to_lanes(b_s), jnp.transpose(s0, (1, 2, 3, 0)),
                           col(r_k), col(lnx_w), col(lnx_b))
    u_s = jnp.transpose(u_t, (2, 0, 1)).reshape(n_db * n_new, D_MODEL)
    return (u_p, u_s), rkvg, s_p, jnp.transpose(s_t, (3, 0, 1, 2)), h_p[:, -1], h_s[:, -1]


def _rope_tables(n_b, seq, n_db, n_new, n_past):
    half = MLA_ROPE // 2
    inv = ROPE_THETA ** (-jnp.arange(half, dtype=F32) / half)
    pos = jnp.concatenate([jnp.tile(jnp.arange(seq, dtype=F32), n_b),
                           jnp.tile(n_past + jnp.arange(n_new, dtype=F32), n_db)])
    ang = pos[:, None] * inv[None, :]
    return jnp.tile(jnp.cos(ang), (1, 4)), jnp.tile(jnp.sin(ang), (1, 4))


def kernel(x_prompt, x_sample, cache_mla_ckv, cache_mla_kpe, cache_fox_k, cache_fox_v, cache_fox_logf,
           state_rwkv_wkv, state_rwkv_shift, page_table, norm_pre, norm_post,
           mla_w_in, mla_q_norm, mla_kv_norm, mla_w_uq, mla_w_ukv, mla_w_o,
           fox_w_in, fox_b_f, fox_q_norm, fox_k_norm, fox_w_o,
           rwkv_mu, rwkv_w_rkvg, rwkv_w0, rwkv_w1, rwkv_w2, rwkv_a0, rwkv_a1, rwkv_a2,
           rwkv_k_k, rwkv_k_a, rwkv_r_k, rwkv_lnx_w, rwkv_lnx_b, rwkv_w_o):
    n_b, seq, _ = x_prompt.shape
    n_db, n_new, _ = x_sample.shape
    n_pages = page_table.shape[1]
    mp, ms = n_b * seq, n_db * n_new
    assert seq % TQ == 0 and seq % TM == 0 and ms % TM == 0 and n_pages % PAGES_PER_STEP == 0
    assert n_new <= 8 and n_new * FOX_HEADS <= 128
    dims = (n_b, seq, n_db, n_new)
    depth = norm_pre.shape[0]
    x = jnp.concatenate([x_prompt.reshape(mp, D_MODEL), x_sample.reshape(ms, D_MODEL)], axis=0)
    tables = _rope_tables(n_b, seq, n_db, n_new, n_pages * PAGE)
    row = lambda a: a.reshape(1, -1)
    outs = {name: [] for name in ("ckv", "kpe", "k", "v", "lf", "wkv_p", "wkv_s", "sh_p", "sh_s")}
    for i in range(depth):
        j, kind = divmod(i, 3)
        gpre = row(norm_pre[i])
        if kind == 0:
            u, gate, ckv, kpe = _mla_layer(x, dims, page_table, cache_mla_ckv, cache_mla_kpe, j, tables, gpre,
                                           mla_w_in[j], row(mla_q_norm[j]), row(mla_kv_norm[j]),
                                           mla_w_uq[j], mla_w_ukv[j])
            w_o = mla_w_o[j]
            outs["ckv"].append(ckv)
            outs["kpe"].append(kpe)
        elif kind == 1:
            u, gate, k, v, lf = _fox_layer(x, dims, page_table, cache_fox_k, cache_fox_v, cache_fox_logf, j, gpre,
                                           fox_w_in[j], fox_b_f[j], fox_q_norm[j], fox_k_norm[j])
            w_o = fox_w_o[j]
            outs["k"].append(k)
            outs["v"].append(v)
            outs["lf"].append(lf)
        else:
            u, gate, s_p, s_s, sh_p, sh_s = _rwkv_layer(
                x, dims, state_rwkv_shift[j], state_rwkv_wkv[j], gpre, rwkv_mu[j], rwkv_w_rkvg[j],
                rwkv_w0[j], rwkv_w1[j], rwkv_w2[j], rwkv_a0[j], rwkv_a1[j], rwkv_a2[j],
                rwkv_k_k[j], rwkv_k_a[j], rwkv_r_k[j].reshape(-1), rwkv_lnx_w[j], rwkv_lnx_b[j])
            w_o = rwkv_w_o[j]
            outs["wkv_p"].append(s_p)
            outs["wkv_s"].append(s_s)
            outs["sh_p"].append(sh_p)
            outs["sh_s"].append(sh_s)
        x = _out_proj(*u, gate, x, w_o.astype(BF16), row(norm_post[i]))

    def split(name, *tail):
        a = jnp.stack(outs[name])
        return (a[:, :mp].reshape((a.shape[0], n_b, seq) + tail),
                a[:, mp:].reshape((a.shape[0], n_db, n_new) + tail))

    p_ckv, s_ckv = split("ckv", MLA_LORA)
    p_kpe, s_kpe = split("kpe", MLA_ROPE)
    p_k, s_k = split("k", FOX_KVH, FOX_HD)
    p_v, s_v = split("v", FOX_KVH, FOX_HD)
    p_lf, s_lf = split("lf", FOX_HEADS)
    return (x[:mp].reshape(n_b, seq, D_MODEL), x[mp:].reshape(n_db, n_new, D_MODEL),
            p_ckv, p_kpe, p_k, p_v, p_lf, jnp.stack(outs["wkv_p"]), jnp.stack(outs["sh_p"]),
            s_ckv, s_kpe, s_k, s_v, s_lf, jnp.stack(outs["wkv_s"]), jnp.stack(outs["sh_s"]))
```

```python
import functools
import math

import jax
import jax.numpy as jnp
from jax import lax
from jax.experimental import pallas as pl
from jax.experimental.pallas import tpu as pltpu

F32 = jnp.float32
BF16 = jnp.bfloat16

D_MODEL = 2048
NORM_EPS = 1e-6
PAGE = 128

MLA_NOPE = 128
MLA_ROPE = 64
MLA_V = 128
MLA_HEADS = 16
MLA_HEAD_GROUP = 4
MLA_LORA = 512
MLA_SCALE = (MLA_NOPE + MLA_ROPE) ** -0.5
MLA_KW = MLA_LORA + 128
ROPE_THETA = 10000.0

FOX_HD = 128
FOX_HEADS = 16
FOX_KVH = 4
FOX_GROUP = 4
FOX_SCALE = FOX_HD ** -0.5

RWKV_HD = 64
RWKV_HEADS = 32
RWKV_LORA_PAD = 128
RWKV_CHUNK = 64
RWKV_SCAN_HEADS = 8
RWKV_LNX_EPS = 64e-5

TM = 256
TQ = 256
PAGES_PER_STEP = 16
DEC_CHAINS = 4
NEG = -1e30
VMEM_LIMIT = 56 * 1024 * 1024
HI = lax.Precision.HIGHEST

NT = (((1,), (1,)), ((), ()))
TN = (((0,), (0,)), ((), ()))


def _params(sem):
    return pltpu.CompilerParams(dimension_semantics=sem, vmem_limit_bytes=VMEM_LIMIT)


def _resident(shape):
    nd = len(shape)
    return pl.BlockSpec(shape, lambda *_: (0,) * nd, pipeline_mode=pl.Buffered(1))


def _dot(a, b):
    return jnp.dot(a, b, preferred_element_type=F32)


def _dot_nt(a, b):
    return lax.dot_general(a, b, NT, preferred_element_type=F32)


def _rms(x, g):
    return x * lax.rsqrt(jnp.mean(x * x, axis=-1, keepdims=True) + NORM_EPS) * g


def _sigmoid(x):
    return 1.0 / (1.0 + jnp.exp(-x))


def _softplus(x):
    return jnp.maximum(x, 0.0) + jnp.log(1.0 + jnp.exp(-jnp.abs(x)))


def _mla_proj_kernel(x_ref, gpre_ref, w_ref, qn_ref, kvn_ref, wuqn_ref, wuqp_ref, wuqr_ref, wk_ref,
                     cos_ref, sin_ref, gate_ref, ckv_ref, kpe_ref, kcat_ref, qfull_ref):
    hb = _rms(x_ref[...], gpre_ref[...]).astype(BF16)
    cq = _rms(_dot(hb, w_ref[:, 0:512]), qn_ref[...]).astype(BF16)
    ckv = _rms(_dot(hb, w_ref[:, 512:1024]), kvn_ref[...])
    z = _dot(hb, w_ref[:, 1024:1152])
    cos_t = cos_ref[...]
    sin_t = sin_ref[...]
    lane = lax.broadcasted_iota(jnp.int32, z.shape, 1)
    kpe = jnp.where(lane < MLA_ROPE, z * cos_t + pltpu.roll(z, 64, 1) * sin_t, 0.0)
    gate_ref[...] = _dot(hb, w_ref[:, 1152:3200])
    ckv_ref[...] = ckv
    kpe_ref[...] = kpe[:, 0:MLA_ROPE]
    kcat_ref[:, 0:512] = ckv.astype(BF16)
    kcat_ref[:, 512:640] = kpe.astype(BF16)
    qn = _dot(cq, wuqn_ref[...]).astype(BF16)
    cos8 = jnp.concatenate([cos_t] * 8, axis=1)
    sin8 = jnp.concatenate([sin_t] * 8, axis=1)
    qpe = (_dot(cq, wuqp_ref[...]) * cos8 + _dot(cq, wuqr_ref[...]) * sin8) * MLA_SCALE
    for h in range(MLA_HEADS):
        ql = _dot(qn[:, h * 128:(h + 1) * 128], wk_ref[h]) * MLA_SCALE
        qfull_ref[h, :, 0:512] = ql.astype(BF16)
        blk = qpe[:, (h // 2) * 128:(h // 2 + 1) * 128]
        if h % 2:
            blk = pltpu.roll(blk, 64, 1)
        qfull_ref[h, :, 512:640] = jnp.where(lane < MLA_ROPE, blk, 0.0).astype(BF16)


def _mla_proj(x, gpre, w_main, qn, kvn, wuqn, wuqp, wuqr, wk, cos_t, sin_t):
    m = x.shape[0]
    row = lambda w: pl.BlockSpec((TM, w), lambda i: (i, 0))
    return pl.pallas_call(
        _mla_proj_kernel,
        grid=(m // TM,),
        in_specs=[row(D_MODEL), _resident((1, D_MODEL)), _resident(w_main.shape), _resident((1, 512)),
                  _resident((1, 512)), _resident(wuqn.shape), _resident(wuqp.shape), _resident(wuqr.shape),
                  _resident(wk.shape), row(128), row(128)],
        out_specs=[row(D_MODEL), row(512), row(MLA_ROPE), row(MLA_KW),
                   pl.BlockSpec((MLA_HEADS, TM, MLA_KW), lambda i: (0, i, 0))],
        out_shape=[jax.ShapeDtypeStruct((m, D_MODEL), F32), jax.ShapeDtypeStruct((m, 512), F32),
                   jax.ShapeDtypeStruct((m, MLA_ROPE), F32), jax.ShapeDtypeStruct((m, MLA_KW), BF16),
                   jax.ShapeDtypeStruct((MLA_HEADS, m, MLA_KW), BF16)],
        compiler_params=_params(("arbitrary",)),
        name="mla_proj",
    )(x, gpre, w_main, qn, kvn, wuqn, wuqp, wuqr, wk, cos_t, sin_t)


def _flash_update(scores, values, m_ref, l_ref, acc_ref, mxu_row_sum=False):
    probs, alphas = {}, {}
    for c, s in scores.items():
        m_prev = m_ref[c]
        m_new = jnp.maximum(m_prev, jnp.max(s, axis=-1, keepdims=True))
        alphas[c] = jnp.exp(m_prev - m_new)
        p = jnp.exp(s - m_new)
        if not mxu_row_sum:
            l_ref[c] = alphas[c] * l_ref[c] + jnp.sum(p, axis=-1, keepdims=True)
        m_ref[c] = m_new
        probs[c] = p.astype(BF16)
    for c in scores:
        acc_ref[c] = alphas[c] * acc_ref[c] + _dot(probs[c], values[c])
        if mxu_row_sum:
            ones = jnp.ones((probs[c].shape[1], 128), BF16)
            l_ref[c] = alphas[c] * l_ref[c] + _dot(probs[c], ones)[:, 0:1]


def _flash_init(m_ref, l_ref, acc_ref):
    m_ref[...] = jnp.full(m_ref.shape, NEG, F32)
    l_ref[...] = jnp.zeros(l_ref.shape, F32)
    acc_ref[...] = jnp.zeros(acc_ref.shape, F32)


def _flash_merge(m_ref, l_ref, acc_ref):
    n = m_ref.shape[0]
    m = functools.reduce(jnp.maximum, [m_ref[c] for c in range(n)])
    w = [jnp.exp(m_ref[c] - m) for c in range(n)]
    return sum(acc_ref[c] * w[c] for c in range(n)) / sum(l_ref[c] * w[c] for c in range(n))


def _flash_scratch(chains, rows, width):
    return [pltpu.VMEM((chains, rows, width), F32), pltpu.VMEM((chains, rows, 1), F32),
            pltpu.VMEM((chains, rows, 1), F32)]


def _mla_attn_prompt_kernel(q_ref, k_ref, wv_ref, o_ref, acc_ref, m_ref, l_ref):
    qi = pl.program_id(1)
    half = MLA_HEAD_GROUP // 2
    rows = half * TQ
    qs = [q_ref[c * half:(c + 1) * half].reshape(rows, MLA_KW) for c in range(2)]
    _flash_init(m_ref, l_ref, acc_ref)

    def keys(kb):
        return k_ref[pl.ds(pl.multiple_of(kb * TQ, TQ), TQ), :]

    def qk(kb):
        k = keys(kb)
        return tuple(_dot_nt(qs[c], k) for c in range(2))

    def update(kb, scores):
        v = keys(kb)[:, 0:MLA_LORA]
        _flash_update(dict(enumerate(scores)), {0: v, 1: v}, m_ref, l_ref, acc_ref)

    def body(kb, scores):
        nxt = qk(kb + 1)
        update(kb, scores)
        return nxt

    scores = lax.fori_loop(0, qi, body, qk(0))
    tok = lax.broadcasted_iota(jnp.int32, (rows, TQ), 0) & (TQ - 1)
    key = lax.broadcasted_iota(jnp.int32, (rows, TQ), 1)
    update(qi, tuple(jnp.where(key <= tok, s, NEG) for s in scores))
    for c in range(2):
        o = (acc_ref[c] / l_ref[c]).astype(BF16)
        for hh in range(half):
            h = c * half + hh
            o_ref[:, h * MLA_V:(h + 1) * MLA_V] = _dot(o[hh * TQ:(hh + 1) * TQ, :], wv_ref[h])


def _mla_attn_prompt(qfull, kcat, wv, n_batch, seq):
    nq = seq // TQ
    hg = MLA_HEAD_GROUP
    rows = hg // 2 * TQ
    return pl.pallas_call(
        _mla_attn_prompt_kernel,
        grid=(n_batch, nq, MLA_HEADS // hg),
        in_specs=[pl.BlockSpec((hg, TQ, MLA_KW), lambda b, q, g: (g, b * nq + q, 0)),
                  pl.BlockSpec((seq, MLA_KW), lambda b, q, g: (b, 0)),
                  pl.BlockSpec((hg, MLA_LORA, MLA_V), lambda b, q, g: (g, 0, 0))],
        out_specs=pl.BlockSpec((TQ, hg * MLA_V), lambda b, q, g: (b * nq + q, g)),
        out_shape=jax.ShapeDtypeStruct((n_batch * seq, D_MODEL), F32),
        scratch_shapes=_flash_scratch(2, rows, MLA_LORA),
        compiler_params=_params(("arbitrary", "arbitrary", "arbitrary")),
        name="mla_attn_prompt",
    )(qfull, kcat, wv)


def _page_copy(pt_ref, pool, buf, sem, layer, first, slot, i):
    return pltpu.make_async_copy(pool.at[layer, pt_ref[first + i]], buf.at[slot, i], sem.at[slot, i])


def _paged_step(pt_ref, pools, bufs, sems, layer, n_steps, consume):
    g_pages = PAGES_PER_STEP
    step = pl.program_id(0) * pl.num_programs(1) + pl.program_id(1)
    slot = step % 2
    copies = lambda s, sl: [_page_copy(pt_ref, pool, buf, sems.at[p], layer, s * g_pages, sl, i)
                            for i in range(g_pages) for p, (pool, buf) in enumerate(zip(pools, bufs))]

    @pl.when(step == 0)
    def _():
        for c in copies(step, slot):
            c.start()

    nxt = jnp.minimum(step + 1, n_steps - 1)
    for c in copies(nxt, 1 - slot):
        c.start()
    per_page = len(pools)
    mine = copies(step, slot)
    for i in range(g_pages):
        for c in mine[i * per_page:(i + 1) * per_page]:
            c.wait()
        consume(i, slot)

    def drain():
        for c in copies(nxt, 1 - slot):
            c.wait()
    return drain


def _mla_dec_kernel(pt_ref, q_ref, knew_ref, ckv_hbm, kpe_hbm, o_ref, cbuf, pbuf, sems,
                    kbuf, kpbuf, acc_ref, m_ref, l_ref, *, layer, n_pages, n_new):
    g_pages = PAGES_PER_STEP
    g = pl.program_id(1)
    n_groups = n_pages // g_pages
    part = g_pages * PAGE // DEC_CHAINS

    @pl.when(g == 0)
    def _():
        _flash_init(m_ref, l_ref, acc_ref)

    def consume(i, slot):
        kbuf[i * PAGE:(i + 1) * PAGE, :] = cbuf[slot, i].astype(BF16)
        kpbuf[:, i * PAGE:(i + 1) * PAGE] = pbuf[slot, i].astype(BF16)

    drain = _paged_step(pt_ref, (ckv_hbm, kpe_hbm), (cbuf, pbuf), sems, layer,
                        pl.num_programs(0) * n_groups, consume)
    q = q_ref[0]
    q_lat = q[:, 0:MLA_LORA]
    q_pe = q[:, MLA_LORA:MLA_LORA + MLA_ROPE]
    keys = {c: kbuf[c * part:(c + 1) * part, :] for c in range(DEC_CHAINS)}
    scores = {c: _dot_nt(q_lat, k) + _dot(q_pe, kpbuf[:, c * part:(c + 1) * part]) for c, k in keys.items()}
    _flash_update(scores, keys, m_ref, l_ref, acc_ref)

    @pl.when(g == n_pages // g_pages - 1)
    def _():
        kn = knew_ref[0]
        s = _dot_nt(q, kn)
        tok = lax.broadcasted_iota(jnp.int32, s.shape, 0) // MLA_HEADS
        key = lax.broadcasted_iota(jnp.int32, s.shape, 1)
        s = jnp.where((key <= tok) & (key < n_new), s, NEG)
        _flash_update({0: s}, {0: kn[:, 0:MLA_LORA]}, m_ref, l_ref, acc_ref)
        o = _flash_merge(m_ref, l_ref, acc_ref)
        o_ref[0] = o[0:n_new * MLA_HEADS, :].reshape(n_new, MLA_HEADS, MLA_LORA)

    @pl.when((pl.program_id(0) == pl.num_programs(0) - 1) & (g == n_groups - 1))
    def _():
        drain()


def _mla_attn_sample(page_table, q_s, knew, cache_ckv, cache_kpe_t, layer, n_new):
    n_b, n_pages = page_table.shape
    g_pages = PAGES_PER_STEP
    pt = page_table.reshape(-1)
    grid_spec = pltpu.PrefetchScalarGridSpec(
        num_scalar_prefetch=1,
        grid=(n_b, n_pages // g_pages),
        in_specs=[pl.BlockSpec((1, 128, MLA_KW), lambda b, g, pt_ref: (b, 0, 0)),
                  pl.BlockSpec((1, 16, MLA_KW), lambda b, g, pt_ref: (b, 0, 0)),
                  pl.BlockSpec(memory_space=pl.ANY), pl.BlockSpec(memory_space=pl.ANY)],
        out_specs=pl.BlockSpec((1, n_new, MLA_HEADS, MLA_LORA), lambda b, g, pt_ref: (b, 0, 0, 0)),
        scratch_shapes=[pltpu.VMEM((2, g_pages, PAGE, MLA_LORA), F32), pltpu.VMEM((2, g_pages, MLA_ROPE, PAGE), F32),
                        pltpu.SemaphoreType.DMA((2, 2, g_pages)),
                        pltpu.VMEM((g_pages * PAGE, MLA_LORA), BF16),
                        pltpu.VMEM((MLA_ROPE, g_pages * PAGE), BF16)] + _flash_scratch(DEC_CHAINS, 128, MLA_LORA),
    )
    return pl.pallas_call(
        functools.partial(_mla_dec_kernel, layer=layer, n_pages=n_pages, n_new=n_new),
        grid_spec=grid_spec,
        out_shape=jax.ShapeDtypeStruct((n_b, n_new, MLA_HEADS, MLA_LORA), F32),
        compiler_params=_params(("arbitrary", "arbitrary")),
        name="mla_attn_sample",
    )(pt, q_s, knew, cache_ckv, cache_kpe_t)


def _mla_up_kernel(o_ref, wv_ref, u_ref):
    for h in range(MLA_HEADS):
        oh = o_ref[:, h * MLA_LORA:(h + 1) * MLA_LORA].astype(BF16)
        u_ref[:, h * MLA_V:(h + 1) * MLA_V] = _dot(oh, wv_ref[h])


def _mla_up(o_lat, wv):
    m = o_lat.shape[0]
    return pl.pallas_call(
        _mla_up_kernel,
        grid=(m // TM,),
        in_specs=[pl.BlockSpec((TM, MLA_HEADS * MLA_LORA), lambda i: (i, 0)), _resident(wv.shape)],
        out_specs=pl.BlockSpec((TM, D_MODEL), lambda i: (i, 0)),
        out_shape=jax.ShapeDtypeStruct((m, D_MODEL), F32),
        compiler_params=_params(("arbitrary",)),
        name="mla_up",
    )(o_lat, wv)


def _head_rms(x, g, scale):
    return x * (lax.rsqrt(jnp.mean(x * x, axis=-1, keepdims=True) + NORM_EPS) * scale) * g


def _fox_proj_kernel(x_ref, gpre_ref, w_ref, bf_ref, qn_ref, kn_ref,
                     q_ref, k_ref, v_ref, kb_ref, vb_ref, lf_ref, c_ref, gate_ref, carry_ref, *, tiles_per_seq):
    i = pl.program_id(0)
    hb = _rms(x_ref[...], gpre_ref[...]).astype(BF16)
    q = _dot(hb, w_ref[:, 0:2048])
    for h in range(FOX_HEADS):
        sl = slice(h * FOX_HD, (h + 1) * FOX_HD)
        q_ref[:, sl] = _head_rms(q[:, sl], qn_ref[...], FOX_SCALE).astype(BF16)
    k = _dot(hb, w_ref[:, 2048:2560])
    for h in range(FOX_KVH):
        sl = slice(h * FOX_HD, (h + 1) * FOX_HD)
        kh = _head_rms(k[:, sl], kn_ref[...], 1.0)
        k_ref[:, sl] = kh
        kb_ref[:, sl] = kh.astype(BF16)
    v = _dot(hb, w_ref[:, 2560:3072])
    v_ref[...] = v
    vb_ref[...] = v.astype(BF16)
    fl = _dot(hb, w_ref[:, 3072:3200]) + bf_ref[...]
    lf = -_softplus(-fl)
    lf_ref[...] = lf[:, 0:FOX_HEADS]

    @pl.when(i % tiles_per_seq == 0)
    def _():
        carry_ref[...] = jnp.zeros(carry_ref.shape, F32)

    r = lax.broadcasted_iota(jnp.int32, (TM, TM), 0)
    c = lax.broadcasted_iota(jnp.int32, (TM, TM), 1)
    tri = (r >= c).astype(F32)
    cum = jnp.dot(tri, lf, preferred_element_type=F32, precision=HI) + carry_ref[...]
    c_ref[...] = cum[:, 0:FOX_HEADS]
    carry_ref[...] = cum[TM - 1:TM, :]
    gate_ref[...] = _dot(hb, w_ref[:, 3200:5248])


def _fox_proj(x, gpre, w_main, bf, qn, kn, tiles_per_seq):
    m = x.shape[0]
    row = lambda w: pl.BlockSpec((TM, w), lambda i: (i, 0))
    kvw = FOX_KVH * FOX_HD
    return pl.pallas_call(
        functools.partial(_fox_proj_kernel, tiles_per_seq=tiles_per_seq),
        grid=(m // TM,),
        in_specs=[row(D_MODEL), _resident((1, D_MODEL)), _resident(w_main.shape), _resident((1, 128)),
                  _resident((1, FOX_HD)), _resident((1, FOX_HD))],
        out_specs=[row(D_MODEL), row(kvw), row(kvw), row(kvw), row(kvw), row(FOX_HEADS), row(FOX_HEADS),
                   row(D_MODEL)],
        out_shape=[jax.ShapeDtypeStruct((m, D_MODEL), BF16), jax.ShapeDtypeStruct((m, kvw), F32),
                   jax.ShapeDtypeStruct((m, kvw), F32), jax.ShapeDtypeStruct((m, kvw), BF16),
                   jax.ShapeDtypeStruct((m, kvw), BF16), jax.ShapeDtypeStruct((m, FOX_HEADS), F32),
                   jax.ShapeDtypeStruct((m, FOX_HEADS), F32), jax.ShapeDtypeStruct((m, D_MODEL), F32)],
        scratch_shapes=[pltpu.VMEM((1, 128), F32)],
        compiler_params=_params(("arbitrary",)),
        name="fox_proj",
    )(x, gpre, w_main, bf, qn, kn)


def _fox_attn_prompt_kernel(q_ref, k_ref, v_ref, c_ref, o_ref, acc_ref, m_ref, l_ref):
    qi = pl.program_id(2)
    half = FOX_GROUP // 2
    rows = half * TQ
    qs = [jnp.concatenate([q_ref[:, g * FOX_HD:(g + 1) * FOX_HD] for g in range(c * half, (c + 1) * half)], axis=0)
          for c in range(2)]
    _flash_init(m_ref, l_ref, acc_ref)

    def qk(kb):
        k = k_ref[pl.ds(pl.multiple_of(kb * TQ, TQ), TQ), :]
        return tuple(_dot_nt(qs[c], k) for c in range(2))

    def update(kb, raw, masked):
        start = pl.multiple_of(kb * TQ, TQ)
        v = v_ref[pl.ds(start, TQ), :]
        cb = c_ref[0, 0, :, pl.ds(start, TQ)]
        scores = {c: (s.reshape(half, TQ, TQ) - cb[c * half:(c + 1) * half, None, :]).reshape(rows, TQ)
                  for c, s in enumerate(raw)}
        if masked:
            tok = lax.broadcasted_iota(jnp.int32, (rows, TQ), 0) & (TQ - 1)
            key = lax.broadcasted_iota(jnp.int32, (rows, TQ), 1)
            scores = {c: jnp.where(key <= tok, s, NEG) for c, s in scores.items()}
        _flash_update(scores, {0: v, 1: v}, m_ref, l_ref, acc_ref, mxu_row_sum=True)

    def body(kb, raw):
        nxt = qk(kb + 1)
        update(kb, raw, False)
        return nxt

    update(qi, lax.fori_loop(0, qi, body, qk(0)), True)
    for c in range(2):
        o = acc_ref[c] / l_ref[c]
        for gg in range(half):
            g = c * half + gg
            o_ref[:, g * FOX_HD:(g + 1) * FOX_HD] = o[gg * TQ:(gg + 1) * TQ, :]


def _fox_attn_prompt(q, kb, vb, c_t, n_batch, seq):
    nq = seq // TQ
    rows = FOX_GROUP // 2 * TQ
    gw = FOX_GROUP * FOX_HD
    return pl.pallas_call(
        _fox_attn_prompt_kernel,
        grid=(n_batch, FOX_KVH, nq),
        in_specs=[pl.BlockSpec((TQ, gw), lambda b, h, qi: (b * nq + qi, h)),
                  pl.BlockSpec((seq, FOX_HD), lambda b, h, qi: (b, h)),
                  pl.BlockSpec((seq, FOX_HD), lambda b, h, qi: (b, h)),
                  pl.BlockSpec((1, 1, FOX_GROUP, seq), lambda b, h, qi: (b, h, 0, 0))],
        out_specs=pl.BlockSpec((TQ, gw), lambda b, h, qi: (b * nq + qi, h)),
        out_shape=jax.ShapeDtypeStruct((n_batch * seq, D_MODEL), F32),
        scratch_shapes=_flash_scratch(2, rows, FOX_HD),
        compiler_params=_params(("arbitrary", "arbitrary", "arbitrary")),
        name="fox_attn_prompt",
    )(q, kb, vb, c_t)


def _lane_cumsum(x):
    lane = lax.broadcasted_iota(jnp.int32, x.shape, 1)
    d = 1
    while d < x.shape[1]:
        x = x + jnp.where(lane >= d, pltpu.roll(x, d, 1), 0.0)
        d *= 2
    return x


def _fox_dec_kernel(pt_ref, q_ref, knew_ref, vnew_ref, lfnew_ref, k_hbm, v_hbm, lf_hbm, o_ref,
                    kraw, vraw, lraw, sems, kbuf, vbuf, lfbuf, carry_ref, acc_ref, m_ref, l_ref,
                    *, layer, n_pages, n_new):
    g_pages = PAGES_PER_STEP
    g = pl.program_id(1)
    n_groups = n_pages // g_pages
    kvw = FOX_KVH * FOX_HD
    part = g_pages * PAGE // DEC_CHAINS

    @pl.when(g == 0)
    def _():
        _flash_init(m_ref, l_ref, acc_ref)
        carry_ref[...] = jnp.zeros(carry_ref.shape, F32)

    def consume(i, slot):
        for h in range(FOX_KVH):
            sl = slice(h * FOX_HD, (h + 1) * FOX_HD)
            kbuf[i * PAGE:(i + 1) * PAGE, sl] = kraw[slot, i, pl.ds(h, PAGE, stride=FOX_KVH), :].astype(BF16)
            vbuf[i * PAGE:(i + 1) * PAGE, sl] = vraw[slot, i, pl.ds(h, PAGE, stride=FOX_KVH), :].astype(BF16)
        lfbuf[i * FOX_HEADS:(i + 1) * FOX_HEADS, :] = lraw[slot, i]

    drain = _paged_step(pt_ref, (k_hbm, v_hbm, lf_hbm), (kraw, vraw, lraw), sems, layer,
                        pl.num_programs(0) * n_groups, consume)

    lf_t = lfbuf[...]
    cum = _lane_cumsum(lf_t)
    tot = jnp.broadcast_to(cum[:, PAGE - 1:PAGE], cum.shape)
    inc = tot
    d = FOX_HEADS
    while d < g_pages * FOX_HEADS:
        inc = inc + jnp.concatenate([jnp.zeros((d, PAGE), F32), inc[:-d, :]], axis=0)
        d *= 2
    carry = carry_ref[...]
    c_keys = cum + (inc - tot) + jnp.concatenate([carry] * g_pages, axis=0)
    carry_ref[...] = carry + inc[(g_pages - 1) * FOX_HEADS:, :]

    q = q_ref[0]
    row_kvh = (lax.broadcasted_iota(jnp.int32, (128, kvw), 0) % FOX_HEADS) // FOX_GROUP
    col_kvh = lax.broadcasted_iota(jnp.int32, (128, kvw), 1) // FOX_HD
    own = row_kvh == col_kvh
    qbd = jnp.where(own, jnp.concatenate([q] * FOX_KVH, axis=1), jnp.zeros((), BF16))
    bias = jnp.concatenate(
        [jnp.concatenate([c_keys[i * FOX_HEADS:(i + 1) * FOX_HEADS, :]] * (128 // FOX_HEADS), axis=0)
         for i in range(g_pages)], axis=1)
    halves = {c: slice(c * part, (c + 1) * part) for c in range(DEC_CHAINS)}
    scores = {c: _dot_nt(qbd, kbuf[ks, :]) for c, ks in halves.items()}
    scores = {c: scores[c] - bias[:, ks] for c, ks in halves.items()}
    _flash_update(scores, {c: vbuf[ks, :] for c, ks in halves.items()}, m_ref, l_ref, acc_ref)

    @pl.when(g == n_pages // g_pages - 1)
    def _():
        c_new = carry_ref[...] + _lane_cumsum(lfnew_ref[0])
        bias_n = jnp.concatenate([c_new] * (128 // FOX_HEADS), axis=0)[:, 0:16]
        s = _dot_nt(qbd, knew_ref[0]) - bias_n
        tok = lax.broadcasted_iota(jnp.int32, s.shape, 0) // FOX_HEADS
        key = lax.broadcasted_iota(jnp.int32, s.shape, 1)
        s = jnp.where((key <= tok) & (key < n_new), s, NEG)
        _flash_update({0: s}, {0: vnew_ref[0]}, m_ref, l_ref, acc_ref)
        o = jnp.where(own, _flash_merge(m_ref, l_ref, acc_ref), 0.0)
        o = o[:, 0:128] + o[:, 128:256] + o[:, 256:384] + o[:, 384:512]
        o_ref[0] = o[0:n_new * FOX_HEADS, :].reshape(n_new, FOX_HEADS, FOX_HD)

    @pl.when((pl.program_id(0) == pl.num_programs(0) - 1) & (g == n_groups - 1))
    def _():
        drain()


def _fox_attn_sample(page_table, q_s, knew, vnew, lfnew_t, cache_k, cache_v, cache_lf_t, layer, n_new):
    n_b, n_pages = page_table.shape
    g_pages = PAGES_PER_STEP
    pt = page_table.reshape(-1)
    kvw = FOX_KVH * FOX_HD
    per_b = lambda shape: pl.BlockSpec((1,) + shape, lambda b, g, pt_ref: (b, 0, 0))
    hbm = pl.BlockSpec(memory_space=pl.ANY)
    grid_spec = pltpu.PrefetchScalarGridSpec(
        num_scalar_prefetch=1,
        grid=(n_b, n_pages // g_pages),
        in_specs=[per_b((128, FOX_HD)), per_b((16, kvw)), per_b((16, kvw)), per_b((FOX_HEADS, 128)), hbm, hbm, hbm],
        out_specs=pl.BlockSpec((1, n_new, FOX_HEADS, FOX_HD), lambda b, g, pt_ref: (b, 0, 0, 0)),
        scratch_shapes=[pltpu.VMEM((2, g_pages, PAGE * FOX_KVH, FOX_HD), F32),
                        pltpu.VMEM((2, g_pages, PAGE * FOX_KVH, FOX_HD), F32),
                        pltpu.VMEM((2, g_pages, FOX_HEADS, PAGE), F32),
                        pltpu.SemaphoreType.DMA((3, 2, g_pages)),
                        pltpu.VMEM((g_pages * PAGE, kvw), BF16), pltpu.VMEM((g_pages * PAGE, kvw), BF16),
                        pltpu.VMEM((g_pages * FOX_HEADS, PAGE), F32),
                        pltpu.VMEM((FOX_HEADS, 128), F32)] + _flash_scratch(DEC_CHAINS, 128, kvw),
    )
    return pl.pallas_call(
        functools.partial(_fox_dec_kernel, layer=layer, n_pages=n_pages, n_new=n_new),
        grid_spec=grid_spec,
        out_shape=jax.ShapeDtypeStruct((n_b, n_new, FOX_HEADS, FOX_HD), F32),
        compiler_params=_params(("arbitrary", "arbitrary")),
        name="fox_attn_sample",
    )(pt, q_s, knew, vnew, lfnew_t, cache_k, cache_v, cache_lf_t)


def _rms_kernel(x_ref, g_ref, h_ref):
    h_ref[...] = _rms(x_ref[...], g_ref[...])


def _rms_norm(x, g):
    m = x.shape[0]
    row = pl.BlockSpec((TM, D_MODEL), lambda i: (i, 0))
    return pl.pallas_call(
        _rms_kernel, grid=(m // TM,), in_specs=[row, _resident((1, D_MODEL))], out_specs=row,
        out_shape=jax.ShapeDtypeStruct((m, D_MODEL), F32), compiler_params=_params(("arbitrary",)),
        name="rms_norm",
    )(x, g)


def _rwkv_mix_kernel(h_ref, hp_ref, mu_ref, w_ref, o_ref):
    h = h_ref[...]
    xm = h + (hp_ref[...] - h) * mu_ref[0]
    o_ref[0] = _dot(xm.astype(BF16), w_ref[0])


def _rwkv_mix_proj(h, hp, mu4, w4):
    m = h.shape[0]
    row = pl.BlockSpec((TM, D_MODEL), lambda p, i: (i, 0))
    return pl.pallas_call(
        _rwkv_mix_kernel,
        grid=(4, m // TM),
        in_specs=[row, row, pl.BlockSpec((1, 1, D_MODEL), lambda p, i: (p, 0, 0)),
                  pl.BlockSpec((1, D_MODEL, D_MODEL), lambda p, i: (p, 0, 0))],
        out_specs=pl.BlockSpec((1, TM, D_MODEL), lambda p, i: (p, i, 0)),
        out_shape=jax.ShapeDtypeStruct((4, m, D_MODEL), F32),
        compiler_params=_params(("arbitrary", "arbitrary")),
        name="rwkv_mix_proj",
    )(h, hp, mu4, w4)


def _split2(x):
    hi = x.astype(BF16)
    return hi, (x - hi.astype(F32)).astype(BF16)


def _head_sums(x):
    r = lax.broadcasted_iota(jnp.int32, (256, 128), 0)
    c = lax.broadcasted_iota(jnp.int32, (256, 128), 1)
    ones2 = ((r & 127) // RWKV_HD == c // RWKV_HD).astype(BF16)
    out = []
    for j in range(x.shape[1] // 128):
        hi, lo = _split2(x[:, j * 128:(j + 1) * 128])
        out.append(_dot(jnp.concatenate([hi, lo], axis=1), ones2))
    return jnp.concatenate(out, axis=1)


def _rwkv_post_kernel(h_ref, hp_ref, r_ref, k_ref, v_ref, muw_ref, mua_ref, w0_ref, w1_ref, w2_ref,
                      a0_ref, a1_ref, a2_ref, kk_ref, ka_ref, rk_ref, *outs, chunked):
    h = h_ref[...]
    dh = hp_ref[...] - h
    xw = (h + dh * muw_ref[...]).astype(BF16)
    xa = (h + dh * mua_ref[...]).astype(BF16)
    wl = w0_ref[...] + _dot(jnp.tanh(_dot(xw, w1_ref[...])).astype(BF16), w2_ref[...])
    lw = -jnp.exp(-_softplus(-wl) - 0.5)
    asig = _sigmoid(a0_ref[...] + _dot(_dot(xa, a1_ref[...]).astype(BF16), a2_ref[...]))
    k = k_ref[0]
    k4 = k * (1.0 + (asig - 1.0) * ka_ref[...])
    kr = k * kk_ref[...]
    kk = kr * lax.rsqrt(jnp.maximum(_head_sums(kr * kr), 1e-24))
    a_vec = -kk
    b_vec = kk * asig
    if not chunked:
        w_ref, k4_ref, a_ref, b_ref = outs
        w_ref[...] = jnp.exp(lw)
        k4_ref[...] = k4
        a_ref[...] = a_vec
        b_ref[...] = b_vec
        return
    x1_ref, x2_ref, bonus_ref, pin_ref = outs
    r = r_ref[0]
    ri = lax.broadcasted_iota(jnp.int32, (TM, TM), 0)
    ci = lax.broadcasted_iota(jnp.int32, (TM, TM), 1)
    tri = ((ri >= ci) & (ri // RWKV_CHUNK == ci // RWKV_CHUNK)).astype(BF16)
    hi = lw.astype(BF16)
    mid, lo = _split2(lw - hi.astype(F32))
    cl =_dot(jnp.concatenate([tri, tri, tri], axis=1), jnp.concatenate([hi, mid, lo], axis=0))
    p_in = jnp.exp(cl)
    p_inv = jnp.exp(-cl)
    at = (a_vec * jnp.exp(cl - lw)).astype(BF16)
    rt = (r * p_in).astype(BF16)
    bt = (b_vec * p_inv).astype(BF16)
    kt = (k4 * p_inv).astype(BF16)
    n = RWKV_CHUNK
    for c in range(TM // n):
        x1_ref[2 * n * c:2 * n * c + n, :] = at[n * c:n * (c + 1), :]
        x1_ref[2 * n * c + n:2 * n * (c + 1), :] = rt[n * c:n * (c + 1), :]
        x2_ref[2 * n * c:2 * n * c + n, :] = bt[n * c:n * (c + 1), :]
        x2_ref[2 * n * c + n:2 * n * (c + 1), :] = kt[n * c:n * (c + 1), :]
    bonus_ref[...] = _head_sums(r * k4 * rk_ref[...]) * v_ref[0]
    pin_ref[...] = p_in


def _rwkv_post(h, hp, rkvg, vecs, w1, w2, a1, a2, row0, n_rows, chunked):
    t0 = row0 // TM
    row = pl.BlockSpec((TM, D_MODEL), lambda i: (i + t0, 0))
    proj = lambda p: pl.BlockSpec((1, TM, D_MODEL), lambda i: (p, i + t0, 0))
    out_row = pl.BlockSpec((TM, D_MODEL), lambda i: (i, 0))
    vec = _resident((1, D_MODEL))
    muw, mua, w0, a0, kk, ka, rk = vecs
    if chunked:
        stacked = pl.BlockSpec((2 * TM, D_MODEL), lambda i: (i, 0))
        out_specs = [stacked, stacked, out_row, out_row]
        out_shape = [jax.ShapeDtypeStruct((2 * n_rows, D_MODEL), BF16)] * 2 + \
                    [jax.ShapeDtypeStruct((n_rows, D_MODEL), F32)] * 2
    else:
        out_specs = [out_row] * 4
        out_shape = [jax.ShapeDtypeStruct((n_rows, D_MODEL), F32)] * 4
    return pl.pallas_call(
        functools.partial(_rwkv_post_kernel, chunked=chunked),
        grid=(n_rows // TM,),
        in_specs=[row, row, proj(0), proj(1), proj(2), vec, vec, vec, _resident(w1.shape), _resident(w2.shape),
                  vec, _resident(a1.shape), _resident(a2.shape), vec, vec, vec],
        out_specs=out_specs,
        out_shape=out_shape,
        compiler_params=_params(("arbitrary",)),
        name="rwkv_post_chunked" if chunked else "rwkv_post_steps",
    )(h, hp, rkvg, rkvg, rkvg, muw, mua, w0, w1, w2, a0, a1, a2, kk, ka, rk)


def _rwkv_chunks(states, x1s, x2s, vs):
    heads = range(len(vs))
    n = vs[0].shape[0]
    row = lax.broadcasted_iota(jnp.int32, (2 * n, 2 * n), 0)
    col = lax.broadcasted_iota(jnp.int32, (2 * n, 2 * n), 1) & (n - 1)
    keep = (row > col) & ((row < n) | (row - n >= col))
    zeros = jnp.zeros((n, RWKV_HD), F32)
    gs = [jnp.where(keep, _dot_nt(x1s[h], x2s[h]), 0.0).astype(BF16) for h in heads]
    zs = [_dot_nt(x1s[h], states[h].astype(BF16)) for h in heads]
    us = [zs[h][0:n] + _dot(gs[h][0:n], jnp.concatenate([zeros, vs[h]], axis=0).astype(BF16)) for h in heads]
    nks = [gs[h][0:n, 0:n] for h in heads]
    levels = int(math.log2(n))
    for lvl in range(levels):
        us = [us[h] + _dot(nks[h], us[h].astype(BF16)) for h in heads]
        if lvl + 1 < levels:
            nks = [_dot(nks[h], nks[h]).astype(BF16) for h in heads]
    uvs = [jnp.concatenate([us[h], vs[h]], axis=0).astype(BF16) for h in heads]
    ys = [zs[h][n:2 * n] + _dot(gs[h][n:2 * n], uvs[h]) for h in heads]
    new_states = [states[h] + lax.dot_general(uvs[h], x2s[h], TN, preferred_element_type=F32) for h in heads]
    return new_states, ys


def _rwkv_scan_kernel(x1_ref, x2_ref, v_ref, bonus_ref, pin_ref, lnw_ref, lnb_ref,
                      u_ref, sout_ref, state_ref, *, heads, n_chunks):
    c = pl.program_id(2)

    @pl.when(c == 0)
    def _():
        state_ref[...] = jnp.zeros(state_ref.shape, F32)

    sls = [slice(h * RWKV_HD, (h + 1) * RWKV_HD) for h in range(heads)]
    new_states, ys = _rwkv_chunks([state_ref[h] for h in range(heads)], [x1_ref[:, sl] for sl in sls],
                                  [x2_ref[:, sl] for sl in sls], [v_ref[0, :, sl] for sl in sls])
    outs = []
    for h, sl in enumerate(sls):
        state_ref[h] = new_states[h] * pin_ref[7:8, sl]
        y = ys[h]
        mean = jnp.mean(y, axis=-1, keepdims=True)
        var = jnp.mean(jnp.square(y - mean), axis=-1, keepdims=True)
        outs.append((y - mean) * lax.rsqrt(var + RWKV_LNX_EPS) * lnw_ref[:, sl] + lnb_ref[:, sl] + bonus_ref[:, sl])
    u_ref[...] = jnp.concatenate(outs, axis=-1)

    @pl.when(c == n_chunks - 1)
    def _():
        sout_ref[0] = state_ref[...]


def _rwkv_scan(x1, x2, rkvg, bonus, pin, lnw, lnb, n_seq, seq):
    n = RWKV_CHUNK
    heads = RWKV_SCAN_HEADS
    n_chunks = seq // n
    w = heads * RWKV_HD
    tok = lambda rows, scale: pl.BlockSpec((rows, w), lambda s, hg, c: (scale * (s * n_chunks + c), hg))
    last8 = pl.BlockSpec((8, w), lambda s, hg, c: ((s * n_chunks + c + 1) * (n // 8) - 1, hg))
    vec = pl.BlockSpec((1, w), lambda s, hg, c: (0, hg))
    return pl.pallas_call(
        functools.partial(_rwkv_scan_kernel, heads=heads, n_chunks=n_chunks),
        grid=(n_seq, RWKV_HEADS // heads, n_chunks),
        in_specs=[tok(2 * n, 1), tok(2 * n, 1),
                  pl.BlockSpec((1, n, w), lambda s, hg, c: (2, s * n_chunks + c, hg)),
                  tok(n, 1), last8, vec, vec],
        out_specs=[tok(n, 1), pl.BlockSpec((1, heads, RWKV_HD, RWKV_HD), lambda s, hg, c: (s, hg, 0, 0))],
        out_shape=[jax.ShapeDtypeStruct((n_seq * seq, D_MODEL), F32),
                   jax.ShapeDtypeStruct((n_seq, RWKV_HEADS, RWKV_HD, RWKV_HD), F32)],
        scratch_shapes=[pltpu.VMEM((heads, RWKV_HD, RWKV_HD), F32)],
        compiler_params=_params(("arbitrary", "arbitrary", "arbitrary")),
        name="rwkv_scan",
    )(x1, x2, rkvg, bonus, pin, lnw, lnb)


def _rwkv_steps_kernel(r_ref, w_ref, k_ref, v_ref, a_ref, b_ref, s0_ref, rk_ref, lnw_ref, lnb_ref,
                       u_ref, sout_ref, y_ref, *, heads, n_new):
    for h in range(heads):
        hs = slice(h * RWKV_HD, (h + 1) * RWKV_HD)

        def body(vi, carry):
            s = s0_ref[h, vi]
            for t in range(n_new):
                sa = jnp.sum(s * a_ref[t, hs, :], axis=0, keepdims=True)
                vv = v_ref[t, pl.ds(h * RWKV_HD + vi, 1), :]
                s = s * w_ref[t, hs, :] + sa * b_ref[t, hs, :] + vv * k_ref[t, hs, :]
                y_ref[t, pl.ds(h * RWKV_HD + vi, 1), :] = jnp.sum(s * r_ref[t, hs, :], axis=0, keepdims=True)
            sout_ref[h, vi] = s
            return carry

        lax.fori_loop(0, RWKV_HD, body, 0)
        for t in range(n_new):
            y = y_ref[t, hs, :]
            mean = jnp.mean(y, axis=0, keepdims=True)
            var = jnp.mean(jnp.square(y - mean), axis=0, keepdims=True)
            yn = (y - mean) * lax.rsqrt(var + RWKV_LNX_EPS) * lnw_ref[hs, :] + lnb_ref[hs, :]
            bonus = jnp.sum(r_ref[t, hs, :] * k_ref[t, hs, :] * rk_ref[hs, :], axis=0, keepdims=True)
            u_ref[t, hs, :] = yn + bonus * v_ref[t, hs, :]


def _rwkv_steps(r, w, k, v, a, b, s0_t, rk, lnw, lnb):
    n_new, _, n_db = r.shape
    heads = 2
    rows = heads * RWKV_HD
    tok = pl.BlockSpec((n_new, rows, n_db), lambda g: (0, g, 0))
    vec = pl.BlockSpec((rows, n_db), lambda g: (g, 0))
    st = pl.BlockSpec((heads, RWKV_HD, RWKV_HD, n_db), lambda g: (g, 0, 0, 0))
    return pl.pallas_call(
        functools.partial(_rwkv_steps_kernel, heads=heads, n_new=n_new),
        grid=(RWKV_HEADS // heads,),
        in_specs=[tok] * 6 + [st, vec, vec, vec],
        out_specs=[tok, st],
        out_shape=[jax.ShapeDtypeStruct((n_new, D_MODEL, n_db), F32),
                   jax.ShapeDtypeStruct((RWKV_HEADS, RWKV_HD, RWKV_HD, n_db), F32)],
        scratch_shapes=[pltpu.VMEM((n_new, rows, n_db), F32)],
        compiler_params=_params(("arbitrary",)),
        name="rwkv_steps",
    )(r, w, k, v, a, b, s0_t, rk, lnw, lnb)


def _out_proj_kernel(up_ref, us_ref, g_ref, x_ref, w_ref, gp_ref, o_ref, *, prompt_tiles):
    g = g_ref[...]
    u = jnp.where(pl.program_id(0) < prompt_tiles, up_ref[...], us_ref[...])
    a = (u * (g * _sigmoid(g))).astype(BF16)
    o_ref[...] = x_ref[...] + _rms(_dot(a, w_ref[...]), gp_ref[...])


def _out_proj(u_prompt, u_sample, gate, x, w_o, gpost):
    m = x.shape[0]
    pt = u_prompt.shape[0] // TM
    row = pl.BlockSpec((TM, D_MODEL), lambda i: (i, 0))
    if gate.ndim == 3:
        gate_spec = pl.BlockSpec((None, TM, D_MODEL), lambda i: (gate.shape[0] - 1, i, 0))
    else:
        gate_spec = row
    return pl.pallas_call(
        functools.partial(_out_proj_kernel, prompt_tiles=pt), grid=(m // TM,),
        in_specs=[pl.BlockSpec((TM, D_MODEL), lambda i: (jnp.minimum(i, pt - 1), 0)),
                  pl.BlockSpec((TM, D_MODEL), lambda i: (jnp.maximum(i - pt, 0), 0)),
                  gate_spec, row, _resident(w_o.shape), _resident((1, D_MODEL))], out_specs=row,
        out_shape=jax.ShapeDtypeStruct((m, D_MODEL), F32), compiler_params=_params(("arbitrary",)),
        name="out_proj",
    )(u_prompt, u_sample, gate, x, w_o, gpost)


def _mla_layer(x, dims, page_table, cache_ckv, cache_kpe, layer, tables, gpre, w_in, q_norm, kv_norm, w_uq, w_ukv):
    n_b, seq, n_db, n_new = dims
    mp = n_b * seq
    w_main = jnp.concatenate(
        [w_in[:, 0:1024], w_in[:, 1024:1088],
         -w_in[:, 1056:1088], w_in[:, 1024:1056], w_in[:, 1088:]], axis=1).astype(BF16)
    uq3 = w_uq.reshape(MLA_LORA, MLA_HEADS, MLA_NOPE + MLA_ROPE)
    wuqn = uq3[:, :, 0:MLA_NOPE].reshape(MLA_LORA, -1).astype(BF16)
    pe = uq3[:, :, MLA_NOPE:]
    wuqp = pe.reshape(MLA_LORA, -1).astype(BF16)
    wuqr = jnp.concatenate([-pe[:, :, 32:], pe[:, :, :32]], axis=-1).reshape(MLA_LORA, -1).astype(BF16)
    wk = jnp.transpose(w_ukv[:, :, 0:MLA_NOPE], (1, 2, 0)).astype(BF16)
    wv = jnp.transpose(w_ukv[:, :, MLA_NOPE:], (1, 0, 2)).astype(BF16)
    gate, ckv, kpe, kcat, qfull = _mla_proj(x, gpre, w_main, q_norm, kv_norm, wuqn, wuqp, wuqr, wk, *tables)
    u_p = _mla_attn_prompt(qfull, kcat, wv, n_b, seq)
    q_s = jnp.transpose(qfull[:, mp:, :], (1, 0, 2)).reshape(n_db, n_new * MLA_HEADS, MLA_KW)
    q_s = jnp.pad(q_s, ((0, 0), (0, 128 - n_new * MLA_HEADS), (0, 0)))
    knew = jnp.pad(kcat[mp:].reshape(n_db, n_new, MLA_KW), ((0, 0), (0, 16 - n_new), (0, 0)))
    o_lat = _mla_attn_sample(page_table, q_s, knew, cache_ckv, jnp.swapaxes(cache_kpe, 2, 3), layer, n_new)
    u_s = _mla_up(o_lat.reshape(n_db * n_new, MLA_HEADS * MLA_LORA), wv)
    return (u_p, u_s), gate, ckv, kpe


def _fox_layer(x, dims, page_table, cache_k, cache_v, cache_lf, layer, gpre, w_in, b_f, q_norm, k_norm):
    n_b, seq, n_db, n_new = dims
    mp = n_b * seq
    kvw = FOX_KVH * FOX_HD
    nq = FOX_HEADS * FOX_HD
    w_main = jnp.concatenate(
        [w_in[:, 0:nq + 2 * kvw], w_in[:, nq + 2 * kvw:nq + 2 * kvw + FOX_HEADS],
         jnp.zeros((D_MODEL, 128 - FOX_HEADS), F32), w_in[:, nq + 2 * kvw + FOX_HEADS:]], axis=1).astype(BF16)
    bf = jnp.pad(b_f, (0, 128 - FOX_HEADS)).reshape(1, 128)
    q, k, v, kb, vb, lf, c, gate = _fox_proj(x, gpre, w_main, bf, q_norm.reshape(1, -1), k_norm.reshape(1, -1),
                                             seq // TM)
    c_t = jnp.transpose(c[:mp].reshape(n_b, seq, FOX_HEADS), (0, 2, 1)).reshape(n_b, FOX_KVH, FOX_GROUP, seq)
    u_p = _fox_attn_prompt(q, kb, vb, c_t, n_b, seq)
    q_s = jnp.pad(q[mp:].reshape(n_db, n_new * FOX_HEADS, FOX_HD), ((0, 0), (0, 128 - n_new * FOX_HEADS), (0, 0)))
    knew = jnp.pad(kb[mp:].reshape(n_db, n_new, kvw), ((0, 0), (0, 16 - n_new), (0, 0)))
    vnew = jnp.pad(vb[mp:].reshape(n_db, n_new, kvw), ((0, 0), (0, 16 - n_new), (0, 0)))
    lfnew_t = jnp.pad(jnp.transpose(lf[mp:].reshape(n_db, n_new, FOX_HEADS), (0, 2, 1)),
                      ((0, 0), (0, 0), (0, 128 - n_new)))
    n_pool = cache_k.shape[1]
    o_s = _fox_attn_sample(page_table, q_s, knew, vnew, lfnew_t,
                           cache_k.reshape(cache_k.shape[0], n_pool, PAGE * FOX_KVH, FOX_HD),
                           cache_v.reshape(cache_v.shape[0], n_pool, PAGE * FOX_KVH, FOX_HD),
                           jnp.swapaxes(cache_lf, 2, 3), layer, n_new)
    return (u_p, o_s.reshape(n_db * n_new, D_MODEL)), gate, k, v, lf


def _rwkv_layer(x, dims, shift0, s0, gpre, mu, w_rkvg, w0, w1, w2, a0, a1, a2, k_k, k_a, r_k, lnx_w, lnx_b):
    n_b, seq, n_db, n_new = dims
    mp = n_b * seq
    vec = lambda a: a.reshape(1, D_MODEL)
    h = _rms_norm(x, gpre)
    h_p = h[:mp].reshape(n_b, seq, D_MODEL)
    h_s = h[mp:].reshape(n_db, n_new, D_MODEL)
    hp = jnp.concatenate(
        [jnp.concatenate([jnp.zeros((n_b, 1, D_MODEL), F32), h_p[:, :-1]], axis=1).reshape(mp, D_MODEL),
         jnp.concatenate([shift0[:, None, :], h_s[:, :-1]], axis=1).reshape(n_db * n_new, D_MODEL)], axis=0)
    rkvg = _rwkv_mix_proj(h, hp, mu[0:4].reshape(4, 1, D_MODEL), w_rkvg.astype(BF16))
    lora = w1.shape[1]
    w1p = jnp.pad(w1, ((0, 0), (0, RWKV_LORA_PAD - lora))).astype(BF16)
    w2p = jnp.pad(w2, ((0, RWKV_LORA_PAD - lora), (0, 0))).astype(BF16)
    a1p = jnp.pad(a1, ((0, 0), (0, RWKV_LORA_PAD - lora))).astype(BF16)
    a2p = jnp.pad(a2, ((0, RWKV_LORA_PAD - lora), (0, 0))).astype(BF16)
    vecs = (vec(mu[4]), vec(mu[5]), vec(w0), vec(a0), vec(k_k), vec(k_a), vec(r_k))
    x1, x2, bonus, pin = _rwkv_post(h, hp, rkvg, vecs, w1p, w2p, a1p, a2p, 0, mp, True)
    u_p, s_p = _rwkv_scan(x1, x2, rkvg, bonus, pin, vec(lnx_w), vec(lnx_b), n_b, seq)
    w_s, k_s, a_s, b_s = _rwkv_post(h, hp, rkvg, vecs, w1p, w2p, a1p, a2p, mp, n_db * n_new, False)
    to_lanes = lambda a: jnp.transpose(a.reshape(n_db, n_new, D_MODEL), (1, 2, 0))
    col = lambda a: jnp.broadcast_to(a.reshape(D_MODEL, 1), (D_MODEL, n_db))
    u_t, s_t = _rwkv_steps(to_lanes(rkvg[0, mp:]), to_lanes(w_s), to_lanes(k_s), to_lanes(rkvg[2, mp:]),
                           to_lanes(a_s), to_lanes(b_s), jnp.transpose(s0, (1, 2, 3, 0)),
                           col(r_k), col(lnx_w), col(lnx_b))
    u_s = jnp.transpose(u_t, (2, 0, 1)).reshape(n_db * n_new, D_MODEL)
    return (u_p, u_s), rkvg, s_p, jnp.transpose(s_t, (3, 0, 1, 2)), h_p[:, -1], h_s[:, -1]


def _rope_tables(n_b, seq, n_db, n_new, n_past):
    half = MLA_ROPE // 2
    inv = ROPE_THETA ** (-jnp.arange(half, dtype=F32) / half)
    pos = jnp.concatenate([jnp.tile(jnp.arange(seq, dtype=F32), n_b),
                           jnp.tile(n_past + jnp.arange(n_new, dtype=F32), n_db)])
    ang = pos[:, None] * inv[None, :]
    return jnp.tile(jnp.cos(ang), (1, 4)), jnp.tile(jnp.sin(ang), (1, 4))


def kernel(x_prompt, x_sample, cache_mla_ckv, cache_mla_kpe, cache_fox_k, cache_fox_v, cache_fox_logf,
           state_rwkv_wkv, state_rwkv_shift, page_table, norm_pre, norm_post,
           mla_w_in, mla_q_norm, mla_kv_norm, mla_w_uq, mla_w_ukv, mla_w_o,
           fox_w_in, fox_b_f, fox_q_norm, fox_k_norm, fox_w_o,
           rwkv_mu, rwkv_w_rkvg, rwkv_w0, rwkv_w1, rwkv_w2, rwkv_a0, rwkv_a1, rwkv_a2,
           rwkv_k_k, rwkv_k_a, rwkv_r_k, rwkv_lnx_w, rwkv_lnx_b, rwkv_w_o):
    n_b, seq, _ = x_prompt.shape
    n_db, n_new, _ = x_sample.shape
    n_pages = page_table.shape[1]
    mp, ms = n_b * seq, n_db * n_new
    assert seq % TQ == 0 and seq % TM == 0 and ms % TM == 0 and n_pages % PAGES_PER_STEP == 0
    assert n_new <= 8 and n_new * FOX_HEADS <= 128
    dims = (n_b, seq, n_db, n_new)
    depth = norm_pre.shape[0]
    x = jnp.concatenate([x_prompt.reshape(mp, D_MODEL), x_sample.reshape(ms, D_MODEL)], axis=0)
    tables = _rope_tables(n_b, seq, n_db, n_new, n_pages * PAGE)
    row = lambda a: a.reshape(1, -1)
    outs = {name: [] for name in ("ckv", "kpe", "k", "v", "lf", "wkv_p", "wkv_s", "sh_p", "sh_s")}
    for i in range(depth):
        j, kind = divmod(i, 3)
        gpre = row(norm_pre[i])
        if kind == 0:
            u, gate, ckv, kpe = _mla_layer(x, dims, page_table, cache_mla_ckv, cache_mla_kpe, j, tables, gpre,
                                           mla_w_in[j], row(mla_q_norm[j]), row(mla_kv_norm[j]),
                                           mla_w_uq[j], mla_w_ukv[j])
            w_o = mla_w_o[j]
            outs["ckv"].append(ckv)
            outs["kpe"].append(kpe)
        elif kind == 1:
            u, gate, k, v, lf = _fox_layer(x, dims, page_table, cache_fox_k, cache_fox_v, cache_fox_logf, j, gpre,
                                           fox_w_in[j], fox_b_f[j], fox_q_norm[j], fox_k_norm[j])
            w_o = fox_w_o[j]
            outs["k"].append(k)
            outs["v"].append(v)
            outs["lf"].append(lf)
        else:
            u, gate, s_p, s_s, sh_p, sh_s = _rwkv_layer(
                x, dims, state_rwkv_shift[j], state_rwkv_wkv[j], gpre, rwkv_mu[j], rwkv_w_rkvg[j],
                rwkv_w0[j], rwkv_w1[j], rwkv_w2[j], rwkv_a0[j], rwkv_a1[j], rwkv_a2[j],
                rwkv_k_k[j], rwkv_k_a[j], rwkv_r_k[j].reshape(-1), rwkv_lnx_w[j], rwkv_lnx_b[j])
            w_o = rwkv_w_o[j]
            outs["wkv_p"].append(s_p)
            outs["wkv_s"].append(s_s)
            outs["sh_p"].append(sh_p)
            outs["sh_s"].append(sh_s)
        x = _out_proj(*u, gate, x, w_o.astype(BF16), row(norm_post[i]))

    def split(name, *tail):
        a = jnp.stack(outs[name])
        return (a[:, :mp].reshape((a.shape[0], n_b, seq) + tail),
                a[:, mp:].reshape((a.shape[0], n_db, n_new) + tail))

    p_ckv, s_ckv = split("ckv", MLA_LORA)
    p_kpe, s_kpe = split("kpe", MLA_ROPE)
    p_k, s_k = split("k", FOX_KVH, FOX_HD)
    p_v, s_v = split("v", FOX_KVH, FOX_HD)
    p_lf, s_lf = split("lf", FOX_HEADS)
    return (x[:mp].reshape(n_b, seq, D_MODEL), x[mp:].reshape(n_db, n_new, D_MODEL),
            p_ckv, p_kpe, p_k, p_v, p_lf, jnp.stack(outs["wkv_p"]), jnp.stack(outs["sh_p"]),
            s_ckv, s_kpe, s_k, s_v, s_lf, jnp.stack(outs["wkv_s"]), jnp.stack(outs["sh_s"]))
```
